```python
import jax, jax.numpy as jnp
from jax import lax
import numpy as np

D_MODEL = 1024
BATCH = 4
SEQ = 4096
DEPTH = 1

CHUNK = 64
RET_HEADS = 4
RET_QK_DIM = 128
RET_V_DIM = 256
RET_QK = RET_HEADS * RET_QK_DIM
RET_V = RET_HEADS * RET_V_DIM
CONV_WIDTH = D_MODEL
CONV_K = 3
D_FF = 4 * D_MODEL
N_MOD = 6
ROPE_BASE = 10000.0
EPS = 1e-6
SPLITS = (RET_QK, RET_QK, RET_V, RET_V, CONV_WIDTH, CONV_WIDTH, CONV_WIDTH, D_MODEL, D_MODEL)
IN_WIDTH = sum(SPLITS)

kernel_name = "hybrid_retention_shortconv_block"


def _rmsnorm(t, w):
    tf = t.astype(jnp.float32)
    y = tf * lax.rsqrt(jnp.mean(tf * tf, axis=-1, keepdims=True) + EPS)
    return (y * w.astype(jnp.float32)).astype(t.dtype)


def _rotary(t, positions):
    d = t.shape[-1]
    inv_freq = ROPE_BASE ** (-jnp.arange(0, d, 2, dtype=jnp.float32) / d)
    ang = positions.astype(jnp.float32)[..., None] * inv_freq
    cos = jnp.cos(ang)[:, :, None, :]
    sin = jnp.sin(ang)[:, :, None, :]
    t1, t2 = t[..., : d // 2], t[..., d // 2:]
    return jnp.concatenate([t1 * cos - t2 * sin, t1 * sin + t2 * cos], axis=-1)


def _retention(q, k, v):
    B, S, H, dk = q.shape
    dv = v.shape[-1]
    nc = S // CHUNK
    log_g = jnp.log1p(-jnp.exp2(-5.0 - jnp.arange(H, dtype=jnp.float32)))
    idx = jnp.arange(CHUNK, dtype=jnp.float32)
    intra_dec = jnp.exp(jnp.abs(idx[:, None] - idx[None, :])[None] * log_g[:, None, None])
    q_dec = jnp.exp((idx + 1.0)[:, None] * log_g[None, :])
    k_dec = jnp.exp((CHUNK - 1.0 - idx)[:, None] * log_g[None, :])
    chunk_dec = jnp.exp(CHUNK * log_g)

    qc = q.reshape(B, nc, CHUNK, H, dk)
    kc = k.reshape(B, nc, CHUNK, H, dk)
    vc = v.reshape(B, nc, CHUNK, H, dv)

    scores = jnp.einsum('bnchd,bnshd->bnhcs', qc, kc) * intra_dec
    o_intra = jnp.einsum('bnhcs,bnshv->bnchv', scores, vc)

    def step(state, inp):
        qi, ki, vi = inp
        o = jnp.einsum('bchk,bhkv->bchv', qi * q_dec[None, :, :, None], state)
        state = chunk_dec[None, :, None, None] * state + jnp.einsum(
            'bchk,bchv->bhkv', ki * k_dec[None, :, :, None], vi)
        return state, o

    xs = (jnp.moveaxis(qc, 1, 0), jnp.moveaxis(kc, 1, 0), jnp.moveaxis(vc, 1, 0))
    s0 = jnp.zeros((B, H, dk, dv), jnp.float32)
    _, o_cross = lax.scan(step, s0, xs)
    o = o_intra + jnp.moveaxis(o_cross, 0, 1)
    return o.reshape(B, S, H, dv)


def _head_groupnorm(o, w):
    mu = jnp.mean(o, axis=-1, keepdims=True)
    var = jnp.mean(jnp.square(o - mu), axis=-1, keepdims=True)
    y = (o - mu) * lax.rsqrt(var + EPS)
    B, S = o.shape[0], o.shape[1]
    return y.reshape(B, S, -1) * w.astype(jnp.float32)


def _causal_conv(z, w):
    S = z.shape[1]
    zp = jnp.pad(z, ((0, 0), (CONV_K - 1, 0), (0, 0)))
    out = zp[:, 0:S] * w[0]
    for j in range(1, CONV_K):
        out = out + zp[:, j:j + S] * w[j]
    return out


def setup_inputs(seed: int = 0) -> dict:
    key = jax.random.key(seed)
    ks = jax.random.split(key, 18)
    f32 = jnp.float32
    nrm = lambda k, shape, s: jax.random.normal(k, shape, f32) * s
    x = jax.random.normal(ks[0], (BATCH, SEQ, D_MODEL), f32)
    c = jax.random.normal(ks[1], (BATCH, D_MODEL), f32)
    positions = jnp.broadcast_to(jnp.arange(SEQ, dtype=jnp.int32)[None, :], (BATCH, SEQ))
    return {
        "x": x,
        "c": c,
        "positions": positions,
        "w_ada": nrm(ks[2], (DEPTH, D_MODEL, N_MOD * D_MODEL), 0.5 * D_MODEL ** -0.5),
        "b_ada": nrm(ks[3], (DEPTH, N_MOD * D_MODEL), 0.01),
        "norm1_w": 1.0 + nrm(ks[4], (DEPTH, D_MODEL), 0.02),
        "w_in": nrm(ks[5], (DEPTH, D_MODEL, IN_WIDTH), D_MODEL ** -0.5),
        "ret_gn_w": 1.0 + nrm(ks[6], (DEPTH, RET_V), 0.02),
        "conv_w": nrm(ks[7], (DEPTH, CONV_K, CONV_WIDTH), CONV_K ** -0.5),
        "w_ret_out": nrm(ks[8], (DEPTH, RET_V, D_MODEL), RET_V ** -0.5),
        "w_conv_out": nrm(ks[9], (DEPTH, CONV_WIDTH, D_MODEL), CONV_WIDTH ** -0.5),
        "w_o": nrm(ks[10], (DEPTH, D_MODEL, D_MODEL), D_MODEL ** -0.5),
        "norm2_w": 1.0 + nrm(ks[11], (DEPTH, D_MODEL), 0.02),
        "w_ff1": nrm(ks[12], (DEPTH, D_MODEL, D_FF), D_MODEL ** -0.5),
        "w_ff2": nrm(ks[13], (DEPTH, D_FF, D_MODEL), D_FF ** -0.5),
        "final_norm_w": 1.0 + nrm(ks[14], (D_MODEL,), 0.02),
    }


def reference(x, c, positions, w_ada, b_ada, norm1_w, w_in, ret_gn_w, conv_w,
              w_ret_out, w_conv_out, w_o, norm2_w, w_ff1, w_ff2, final_norm_w):
    B, S, _ = x.shape
    split_points = [int(i) for i in np.cumsum(SPLITS)[:-1]]
    h = x
    for l in range(DEPTH):
        mod = jnp.einsum('bd,de->be', jax.nn.silu(c), w_ada[l]) + b_ada[l]
        sh1, sc1, g1, sh2, sc2, g2 = jnp.split(mod[:, None, :], N_MOD, axis=-1)

        u = _rmsnorm(h, norm1_w[l]) * (1.0 + sc1) + sh1
        proj = jnp.einsum('bsd,de->bse', u, w_in[l])
        q, k, v, og, cb, cc, cx, ga, gb = jnp.split(proj, split_points, axis=-1)

        qh = _rotary(q.astype(jnp.float32).reshape(B, S, RET_HEADS, RET_QK_DIM), positions)
        kh = _rotary(k.astype(jnp.float32).reshape(B, S, RET_HEADS, RET_QK_DIM), positions) * (RET_QK_DIM ** -0.5)
        vh = v.astype(jnp.float32).reshape(B, S, RET_HEADS, RET_V_DIM)
        ret = _head_groupnorm(_retention(qh, kh, vh), ret_gn_w[l]).astype(x.dtype)
        y_ret = jnp.einsum('bse,ed->bsd', jax.nn.silu(og) * ret, w_ret_out[l])

        y_conv = jnp.einsum('bse,ed->bsd', cb * _causal_conv(cc * cx, conv_w[l]), w_conv_out[l])

        merged = jax.nn.sigmoid(ga) * y_ret + jax.nn.sigmoid(gb) * y_conv
        h = h + g1 * jnp.einsum('bsd,de->bse', merged, w_o[l])

        u2 = _rmsnorm(h, norm2_w[l]) * (1.0 + sc2) + sh2
        hid = jnp.square(jax.nn.relu(jnp.einsum('bsd,df->bsf', u2, w_ff1[l])))
        h = h + g2 * jnp.einsum('bsf,fd->bsd', hid, w_ff2[l])
    return _rmsnorm(h, final_norm_w)
```

```python
import functools

import jax
import jax.numpy as jnp
import numpy as np
from jax import lax
from jax.experimental import pallas as pl
from jax.experimental.pallas import tpu as pltpu

CHUNK = 64
RET_HEADS = 4
RET_QK_DIM = 128
RET_V_DIM = 256
CONV_K = 3
N_MOD = 6
ROPE_BASE = 10000.0
EPS = 1e-6

MIXER_TILE = 256
MLP_TILE = 512
SUBLANES = 8
VMEM_LIMIT_BYTES = 56 * 1024 * 1024

_BF16 = jnp.bfloat16
_F32 = jnp.float32


def _const_spec(shape):
    nd = len(shape)
    return pl.BlockSpec(shape, lambda *_: (0,) * nd, pipeline_mode=pl.Buffered(1))


def _dot(a, b):
    return jnp.dot(a, b, preferred_element_type=_F32)


def _ada_kernel(c_ref, w_ref, b_ref, o_ref):
    o_ref[...] = _dot(jax.nn.silu(c_ref[...]), w_ref[...]) + b_ref[...]


def _ada(c, w_ada, b_ada):
    bsz, d = c.shape
    n = w_ada.shape[1]
    return pl.pallas_call(
        _ada_kernel,
        out_shape=jax.ShapeDtypeStruct((bsz, n), _F32),
        grid=(n // d,),
        in_specs=[
            pl.BlockSpec((bsz, d), lambda j: (0, 0)),
            pl.BlockSpec((d, d), lambda j: (0, j)),
            pl.BlockSpec((1, d), lambda j: (0, j)),
        ],
        out_specs=pl.BlockSpec((bsz, d), lambda j: (0, j)),
        compiler_params=pltpu.CompilerParams(dimension_semantics=("arbitrary",)),
        name="ada",
    )(c, w_ada, b_ada.reshape(1, n))


def _rmsnorm_mod(t, w, scale, shift):
    ms = jnp.mean(t * t, axis=-1, keepdims=True)
    return t * lax.rsqrt(ms + EPS) * (w * (1.0 + scale)) + shift


def _mixer_kernel(x_ref, mod_ref, pos_ref, n1w_ref, gnw_ref, convw_ref,
                  invf_ref, sgn_ref, dmask_ref, qdec_ref, kdec_ref, tdec_ref,
                  w_in_ref, w_ro_ref, w_co_ref, w_o_ref,
                  h_ref,
                  state_ref, zbuf_ref, gated_ref, *, d_model):
    tt = x_ref.shape[1]
    d = d_model
    qk = RET_HEADS * RET_QK_DIM
    rv = RET_HEADS * RET_V_DIM
    o_q, o_k, o_v = 0, qk, 2 * qk
    o_og = o_v + rv
    o_cb = o_og + rv
    o_cc = o_cb + d
    o_cx = o_cc + d
    o_ga = o_cx + d
    o_gb = o_ga + d

    @pl.when(pl.program_id(1) == 0)
    def _():
        state_ref[...] = jnp.zeros_like(state_ref)
        zbuf_ref[0:SUBLANES, :] = jnp.zeros((SUBLANES, d), _F32)

    x = x_ref[0]
    mod = mod_ref[0]
    sh1, sc1, g1 = mod[0:1], mod[1:2], mod[2:3]
    ub = _rmsnorm_mod(x, n1w_ref[...], sc1, sh1).astype(_BF16)

    def proj(lo, width):
        return _dot(ub, w_in_ref[:, lo:lo + width])

    ang = pos_ref[0].astype(_F32) * invf_ref[...]
    cosf = jnp.cos(ang)
    sinf = jnp.sin(ang) * sgn_ref[...]
    kscale = RET_QK_DIM ** -0.5
    cosk = cosf * kscale
    sink = sinf * kscale

    q = proj(o_q, qk)
    k = proj(o_k, qk)
    v = proj(o_v, rv)
    og = proj(o_og, rv)

    half = RET_QK_DIM // 2
    for h in range(RET_HEADS):
        qh = q[:, h * RET_QK_DIM:(h + 1) * RET_QK_DIM]
        kh = k[:, h * RET_QK_DIM:(h + 1) * RET_QK_DIM]
        vh = v[:, h * RET_V_DIM:(h + 1) * RET_V_DIM].astype(_BF16)
        qr = qh * cosf + pltpu.roll(qh, half, 1) * sinf
        kr = kh * cosk + pltpu.roll(kh, half, 1) * sink
        s = lax.dot_general(qr.astype(_BF16), kr.astype(_BF16),
                            (((1,), (1,)), ((), ())),
                            preferred_element_type=_F32) * dmask_ref[h]
        st = state_ref[h]
        o = _dot(s.astype(_BF16), vh) + _dot((qr * qdec_ref[h]).astype(_BF16),
                                            st.astype(_BF16))
        state_ref[h] = tdec_ref[h] * st + lax.dot_general(
            (kr * kdec_ref[h]).astype(_BF16), vh, (((0,), (0,)), ((), ())),
            preferred_element_type=_F32)
        mu = jnp.mean(o, axis=-1, keepdims=True)
        oc = o - mu
        var = jnp.mean(oc * oc, axis=-1, keepdims=True)
        lo = h * RET_V_DIM
        y = oc * lax.rsqrt(var + EPS) * gnw_ref[:, lo:lo + RET_V_DIM]
        gated_ref[:, lo:lo + RET_V_DIM] = (
            jax.nn.silu(og[:, lo:lo + RET_V_DIM]) * y).astype(_BF16)
    y_ret = _dot(gated_ref[...], w_ro_ref[...])

    z = proj(o_cc, d) * proj(o_cx, d)
    zbuf_ref[SUBLANES:SUBLANES + tt, :] = z
    z1 = zbuf_ref[SUBLANES - 1:SUBLANES - 1 + tt, :]
    z2 = zbuf_ref[SUBLANES - 2:SUBLANES - 2 + tt, :]
    zbuf_ref[0:SUBLANES, :] = z[tt - SUBLANES:tt, :]
    conv = z2 * convw_ref[0:1, :] + z1 * convw_ref[1:2, :] + z * convw_ref[2:3, :]
    y_conv = _dot((proj(o_cb, d) * conv).astype(_BF16), w_co_ref[...])

    merged = (jax.nn.sigmoid(proj(o_ga, d)) * y_ret
              + jax.nn.sigmoid(proj(o_gb, d)) * y_conv)
    h_ref[0] = x + g1 * _dot(merged.astype(_BF16), w_o_ref[...])


def _decay_tables(tile):
    hh = jnp.arange(RET_HEADS, dtype=_F32)
    log_g = jnp.log1p(-jnp.exp2(-5.0 - hh))
    idx = jnp.arange(tile, dtype=_F32)
    diff = idx[:, None] - idx[None, :]
    cn = jnp.arange(tile)[:, None] // CHUNK
    cm = jnp.arange(tile)[None, :] // CHUNK
    expo = jnp.where(cn == cm, jnp.abs(diff), diff)
    dmask = jnp.where((cm <= cn)[None],
                      jnp.exp(expo[None] * log_g[:, None, None]), 0.0)
    qdec = jnp.exp((idx + 1.0)[None, :] * log_g[:, None])
    kdec = jnp.exp((tile - 1.0 - idx)[None, :] * log_g[:, None])
    tdec = jnp.exp(tile * log_g)
    bcast = lambda t: jnp.broadcast_to(t[:, :, None], (RET_HEADS, tile, RET_QK_DIM))
    return dmask, bcast(qdec), bcast(kdec), tdec


def _mixer(x, mod, positions, norm1_w, ret_gn_w, conv_w, w_in, w_ret_out,
           w_conv_out, w_o):
    bsz, seq, d = x.shape
    tt = MIXER_TILE
    qk = RET_HEADS * RET_QK_DIM
    rv = RET_HEADS * RET_V_DIM
    half = RET_QK_DIM // 2
    inv_freq = ROPE_BASE ** (-jnp.arange(0, RET_QK_DIM, 2, dtype=_F32) / RET_QK_DIM)
    invf = jnp.concatenate([inv_freq, inv_freq]).reshape(1, RET_QK_DIM)
    sgn = jnp.concatenate([-jnp.ones((half,), _F32),
                           jnp.ones((half,), _F32)]).reshape(1, RET_QK_DIM)
    dmask, qdec, kdec, tdec = _decay_tables(tt)
    in_width = w_in.shape[1]

    tile_spec = pl.BlockSpec((1, tt, d), lambda b, t: (b, t, 0))
    return pl.pallas_call(
        functools.partial(_mixer_kernel, d_model=d),
        out_shape=jax.ShapeDtypeStruct((bsz, seq, d), _F32),
        grid=(bsz, seq // tt),
        in_specs=[
            tile_spec,
            pl.BlockSpec((1, N_MOD, d), lambda b, t: (b, 0, 0)),
            pl.BlockSpec((1, tt, 1), lambda b, t: (b, t, 0)),
            _const_spec((1, d)),
            _const_spec((1, rv)),
            _const_spec((CONV_K, d)),
            _const_spec((1, RET_QK_DIM)),
            _const_spec((1, RET_QK_DIM)),
            _const_spec((RET_HEADS, tt, tt)),
            _const_spec((RET_HEADS, tt, RET_QK_DIM)),
            _const_spec((RET_HEADS, tt, RET_QK_DIM)),
            pl.BlockSpec(memory_space=pltpu.SMEM),
            _const_spec((d, in_width)),
            _const_spec((rv, d)),
            _const_spec((d, d)),
            _const_spec((d, d)),
        ],
        out_specs=tile_spec,
        scratch_shapes=[
            pltpu.VMEM((RET_HEADS, RET_QK_DIM, RET_V_DIM), _F32),
            pltpu.VMEM((SUBLANES + tt, d), _F32),
            pltpu.VMEM((tt, rv), _BF16),
        ],
        compiler_params=pltpu.CompilerParams(
            dimension_semantics=("arbitrary", "arbitrary"),
            vmem_limit_bytes=VMEM_LIMIT_BYTES),
        name="mixer",
    )(x, mod, positions.reshape(bsz, seq, 1), norm1_w.reshape(1, d),
      ret_gn_w.reshape(1, rv), conv_w, invf, sgn, dmask, qdec, kdec, tdec,
      w_in.astype(_BF16), w_ret_out.astype(_BF16), w_conv_out.astype(_BF16),
      w_o.astype(_BF16))


def _mlp_kernel(h_ref, mod_ref, n2w_ref, fw_ref, w1_ref, w2_ref, o_ref, *, final_norm):
    h = h_ref[0]
    mod = mod_ref[0]
    sh2, sc2, g2 = mod[3:4], mod[4:5], mod[5:6]
    u2 = _rmsnorm_mod(h, n2w_ref[...], sc2, sh2).astype(_BF16)
    hid = jnp.square(jnp.maximum(_dot(u2, w1_ref[...]), 0.0)).astype(_BF16)
    h2 = h + g2 * _dot(hid, w2_ref[...])
    if final_norm:
        ms = jnp.mean(h2 * h2, axis=-1, keepdims=True)
        h2 = h2 * lax.rsqrt(ms + EPS) * fw_ref[...]
    o_ref[0] = h2


def _mlp(h, mod, norm2_w, final_norm_w, w_ff1, w_ff2, final_norm):
    bsz, seq, d = h.shape
    tm = MLP_TILE
    dff = w_ff1.shape[1]
    tile_spec = pl.BlockSpec((1, tm, d), lambda b, t: (b, t, 0))
    return pl.pallas_call(
        functools.partial(_mlp_kernel, final_norm=final_norm),
        out_shape=jax.ShapeDtypeStruct((bsz, seq, d), _F32),
        grid=(bsz, seq // tm),
        in_specs=[
            tile_spec,
            pl.BlockSpec((1, N_MOD, d), lambda b, t: (b, 0, 0)),
            _const_spec((1, d)),
            _const_spec((1, d)),
            _const_spec((d, dff)),
            _const_spec((dff, d)),
        ],
        out_specs=tile_spec,
        compiler_params=pltpu.CompilerParams(
            dimension_semantics=("arbitrary", "arbitrary"),
            vmem_limit_bytes=VMEM_LIMIT_BYTES),
        name="mlp",
    )(h, mod, norm2_w.reshape(1, d), final_norm_w.reshape(1, d),
      w_ff1.astype(_BF16), w_ff2.astype(_BF16))


def kernel(x, c, positions, w_ada, b_ada, norm1_w, w_in, ret_gn_w, conv_w, w_ret_out, w_conv_out, w_o, norm2_w, w_ff1, w_ff2, final_norm_w):
    bsz, seq, d = x.shape
    depth = w_ada.shape[0]
    assert seq % MIXER_TILE == 0 and seq % MLP_TILE == 0 and MIXER_TILE % CHUNK == 0
    h = x
    for l in range(depth):
        mod = _ada(c, w_ada[l], b_ada[l]).reshape(bsz, N_MOD, d)
        h = _mixer(h, mod, positions, norm1_w[l], ret_gn_w[l], conv_w[l],
                   w_in[l], w_ret_out[l], w_conv_out[l], w_o[l])
        h = _mlp(h, mod, norm2_w[l], final_norm_w, w_ff1[l], w_ff2[l],
                 final_norm=(l == depth - 1))
    return h
```

```python
import functools

import jax
import jax.numpy as jnp
from jax import lax
from jax.experimental import pallas as pl
from jax.experimental.pallas import tpu as pltpu

CHUNK = 64
RET_HEADS = 4
RET_QK_DIM = 128
RET_V_DIM = 256
CONV_K = 3
N_MOD = 6
ROPE_BASE = 10000.0
EPS = 1e-6

MIXER_TILE = 512
MLP_TILE = 1024
ROW_BLOCK = 256
SUBLANES = 8
VMEM_LIMIT_BYTES = 56 * 1024 * 1024

_BF16 = jnp.bfloat16
_F32 = jnp.float32


def _const_spec(shape):
    nd = len(shape)
    return pl.BlockSpec(shape, lambda *_: (0,) * nd, pipeline_mode=pl.Buffered(1))


def _dot(a, b):
    return jnp.dot(a, b, preferred_element_type=_F32)


def _ada_kernel(c_ref, w_ref, b_ref, o_ref):
    o_ref[...] = _dot(jax.nn.silu(c_ref[...]), w_ref[...]) + b_ref[...]


def _ada(c, w_ada, b_ada):
    bsz, d = c.shape
    n = w_ada.shape[1]
    return pl.pallas_call(
        _ada_kernel,
        out_shape=jax.ShapeDtypeStruct((bsz, n), _F32),
        grid=(n // d,),
        in_specs=[
            pl.BlockSpec((bsz, d), lambda j: (0, 0)),
            pl.BlockSpec((d, d), lambda j: (0, j)),
            pl.BlockSpec((1, d), lambda j: (0, j)),
        ],
        out_specs=pl.BlockSpec((bsz, d), lambda j: (0, j)),
        compiler_params=pltpu.CompilerParams(dimension_semantics=("arbitrary",)),
        name="ada",
    )(c, w_ada, b_ada.reshape(1, n))


def _rmsnorm_mod(t, gain, shift):
    ms = jnp.mean(t * t, axis=-1, keepdims=True)
    return t * lax.rsqrt(ms + EPS) * gain + shift


def _mixer_kernel(x_ref, mod_ref, pos_ref, n1w_ref, gnw_ref, convw_ref,
                  invf_ref, sgn_ref, dmask_ref, qdec_ref, kdec_ref, tdec_ref,
                  w_in_ref, w_ro_ref, w_co_ref, w_o_ref,
                  h_ref,
                  state_ref, zbuf_ref, gated_ref, *, d_model):
    tt = x_ref.shape[1]
    d = d_model
    qk = RET_HEADS * RET_QK_DIM
    rv = RET_HEADS * RET_V_DIM
    o_q, o_k, o_v = 0, qk, 2 * qk
    o_og = o_v + rv
    o_cb = o_og + rv
    o_cc = o_cb + d
    o_cx = o_cc + d
    o_ga = o_cx + d
    o_gb = o_ga + d

    @pl.when(pl.program_id(1) == 0)
    def _():
        state_ref[...] = jnp.zeros_like(state_ref)
        zbuf_ref[0:SUBLANES, :] = jnp.zeros((SUBLANES, d), _F32)

    mod = mod_ref[0]
    sh1, sc1, g1 = mod[0:1], mod[1:2], mod[2:3]
    gain1 = n1w_ref[...] * (1.0 + sc1)
    kscale = RET_QK_DIM ** -0.5
    half = RET_QK_DIM // 2

    for r0 in range(0, tt, ROW_BLOCK):
        rows = slice(r0, r0 + ROW_BLOCK)
        x = x_ref[0, rows, :]
        ub = _rmsnorm_mod(x, gain1, sh1).astype(_BF16)

        def proj(lo, width):
            return _dot(ub, w_in_ref[:, lo:lo + width])

        ang = pos_ref[0, rows, :].astype(_F32) * invf_ref[...]
        cosf = jnp.cos(ang)
        sinf = jnp.sin(ang) * sgn_ref[...]
        cosk = cosf * kscale
        sink = sinf * kscale

        q = proj(o_q, qk)
        k = proj(o_k, qk)
        v = proj(o_v, rv)
        og = proj(o_og, rv)

        for h in range(RET_HEADS):
            qcols = slice(h * RET_QK_DIM, (h + 1) * RET_QK_DIM)
            vcols = slice(h * RET_V_DIM, (h + 1) * RET_V_DIM)
            qh = q[:, qcols]
            kh = k[:, qcols]
            vh = v[:, vcols].astype(_BF16)
            qr = qh * cosf + pltpu.roll(qh, half, 1) * sinf
            kr = kh * cosk + pltpu.roll(kh, half, 1) * sink
            s = lax.dot_general(qr.astype(_BF16), kr.astype(_BF16),
                                (((1,), (1,)), ((), ())),
                                preferred_element_type=_F32) * dmask_ref[h]
            st = state_ref[h]
            o = _dot(s.astype(_BF16), vh) + _dot(
                (qr * qdec_ref[h]).astype(_BF16), st.astype(_BF16))
            state_ref[h] = tdec_ref[h] * st + lax.dot_general(
                (kr * kdec_ref[h]).astype(_BF16), vh, (((0,), (0,)), ((), ())),
                preferred_element_type=_F32)
            mu = jnp.mean(o, axis=-1, keepdims=True)
            oc = o - mu
            var = jnp.mean(oc * oc, axis=-1, keepdims=True)
            y = oc * lax.rsqrt(var + EPS) * gnw_ref[:, vcols]
            gated_ref[rows, vcols] = (jax.nn.silu(og[:, vcols]) * y).astype(_BF16)
        y_ret = _dot(gated_ref[rows, :], w_ro_ref[...])

        z = proj(o_cc, d) * proj(o_cx, d)
        z0 = SUBLANES + r0
        zbuf_ref[z0:z0 + ROW_BLOCK, :] = z
        z1 = zbuf_ref[z0 - 1:z0 - 1 + ROW_BLOCK, :]
        z2 = zbuf_ref[z0 - 2:z0 - 2 + ROW_BLOCK, :]
        conv = z2 * convw_ref[0:1, :] + z1 * convw_ref[1:2, :] + z * convw_ref[2:3, :]
        y_conv = _dot((proj(o_cb, d) * conv).astype(_BF16), w_co_ref[...])

        merged = (jax.nn.sigmoid(proj(o_ga, d)) * y_ret
                  + jax.nn.sigmoid(proj(o_gb, d)) * y_conv)
        h_ref[0, rows, :] = x + g1 * _dot(merged.astype(_BF16), w_o_ref[...])
    zbuf_ref[0:SUBLANES, :] = zbuf_ref[tt:tt + SUBLANES, :]


def _decay_tables(tile):
    hh = jnp.arange(RET_HEADS, dtype=_F32)
    log_g = jnp.log1p(-jnp.exp2(-5.0 - hh))
    idx = jnp.arange(tile, dtype=_F32)
    diff = idx[:, None] - idx[None, :]
    cn = jnp.arange(tile)[:, None] // CHUNK
    cm = jnp.arange(tile)[None, :] // CHUNK
    expo = jnp.where(cn == cm, jnp.abs(diff), diff)
    dmask = jnp.where((cm <= cn)[None],
                      jnp.exp(expo[None] * log_g[:, None, None]), 0.0)
    qdec = jnp.exp((idx + 1.0)[None, :] * log_g[:, None])
    kdec = jnp.exp((tile - 1.0 - idx)[None, :] * log_g[:, None])
    tdec = jnp.exp(tile * log_g)
    bcast = lambda t: jnp.broadcast_to(t[:, :, None], (RET_HEADS, tile, RET_QK_DIM))
    return dmask, bcast(qdec), bcast(kdec), tdec


def _mixer(x, mod, positions, norm1_w, ret_gn_w, conv_w, w_in, w_ret_out,
           w_conv_out, w_o):
    bsz, seq, d = x.shape
    tt = MIXER_TILE
    rv = RET_HEADS * RET_V_DIM
    half = RET_QK_DIM // 2
    inv_freq = ROPE_BASE ** (-jnp.arange(0, RET_QK_DIM, 2, dtype=_F32) / RET_QK_DIM)
    invf = jnp.concatenate([inv_freq, inv_freq]).reshape(1, RET_QK_DIM)
    sgn = jnp.concatenate([-jnp.ones((half,), _F32),
                           jnp.ones((half,), _F32)]).reshape(1, RET_QK_DIM)
    dmask, qdec, kdec, tdec = _decay_tables(ROW_BLOCK)
    in_width = w_in.shape[1]

    tile_spec = pl.BlockSpec((1, tt, d), lambda b, t: (b, t, 0))
    return pl.pallas_call(
        functools.partial(_mixer_kernel, d_model=d),
        out_shape=jax.ShapeDtypeStruct((bsz, seq, d), _F32),
        grid=(bsz, seq // tt),
        in_specs=[
            tile_spec,
            pl.BlockSpec((1, N_MOD, d), lambda b, t: (b, 0, 0)),
            pl.BlockSpec((1, tt, 1), lambda b, t: (b, t, 0)),
            _const_spec((1, d)),
            _const_spec((1, rv)),
            _const_spec((CONV_K, d)),
            _const_spec((1, RET_QK_DIM)),
            _const_spec((1, RET_QK_DIM)),
            _const_spec((RET_HEADS, ROW_BLOCK, ROW_BLOCK)),
            _const_spec((RET_HEADS, ROW_BLOCK, RET_QK_DIM)),
            _const_spec((RET_HEADS, ROW_BLOCK, RET_QK_DIM)),
            pl.BlockSpec(memory_space=pltpu.SMEM),
            _const_spec((d, in_width)),
            _const_spec((rv, d)),
            _const_spec((d, d)),
            _const_spec((d, d)),
        ],
        out_specs=tile_spec,
        scratch_shapes=[
            pltpu.VMEM((RET_HEADS, RET_QK_DIM, RET_V_DIM), _F32),
            pltpu.VMEM((SUBLANES + tt, d), _F32),
            pltpu.VMEM((tt, rv), _BF16),
        ],
        compiler_params=pltpu.CompilerParams(
            dimension_semantics=("arbitrary", "arbitrary"),
            vmem_limit_bytes=VMEM_LIMIT_BYTES),
        name="mixer",
    )(x, mod, positions.reshape(bsz, seq, 1), norm1_w.reshape(1, d),
      ret_gn_w.reshape(1, rv), conv_w, invf, sgn, dmask, qdec, kdec, tdec,
      w_in.astype(_BF16), w_ret_out.astype(_BF16), w_conv_out.astype(_BF16),
      w_o.astype(_BF16))


def _mlp_kernel(h_ref, mod_ref, n2w_ref, fw_ref, w1_ref, w2_ref, o_ref, *, final_norm):
    mod = mod_ref[0]
    sh2, sc2, g2 = mod[3:4], mod[4:5], mod[5:6]
    gain2 = n2w_ref[...] * (1.0 + sc2)
    for r0 in range(0, h_ref.shape[1], ROW_BLOCK):
        rows = slice(r0, r0 + ROW_BLOCK)
        h = h_ref[0, rows, :]
        u2 = _rmsnorm_mod(h, gain2, sh2).astype(_BF16)
        hid = jnp.square(jnp.maximum(_dot(u2, w1_ref[...]), 0.0)).astype(_BF16)
        h2 = h + g2 * _dot(hid, w2_ref[...])
        if final_norm:
            ms = jnp.mean(h2 * h2, axis=-1, keepdims=True)
            h2 = h2 * lax.rsqrt(ms + EPS) * fw_ref[...]
        o_ref[0, rows, :] = h2


def _mlp(h, mod, norm2_w, final_norm_w, w_ff1, w_ff2, final_norm):
    bsz, seq, d = h.shape
    tm = MLP_TILE
    dff = w_ff1.shape[1]
    tile_spec = pl.BlockSpec((1, tm, d), lambda b, t: (b, t, 0))
    return pl.pallas_call(
        functools.partial(_mlp_kernel, final_norm=final_norm),
        out_shape=jax.ShapeDtypeStruct((bsz, seq, d), _F32),
        grid=(bsz, seq // tm),
        in_specs=[
            tile_spec,
            pl.BlockSpec((1, N_MOD, d), lambda b, t: (b, 0, 0)),
            _const_spec((1, d)),
            _const_spec((1, d)),
            _const_spec((d, dff)),
            _const_spec((dff, d)),
        ],
        out_specs=tile_spec,
        compiler_params=pltpu.CompilerParams(
            dimension_semantics=("arbitrary", "arbitrary"),
            vmem_limit_bytes=VMEM_LIMIT_BYTES),
        name="mlp",
    )(h, mod, norm2_w.reshape(1, d), final_norm_w.reshape(1, d),
      w_ff1.astype(_BF16), w_ff2.astype(_BF16))


def kernel(x, c, positions, w_ada, b_ada, norm1_w, w_in, ret_gn_w, conv_w, w_ret_out, w_conv_out, w_o, norm2_w, w_ff1, w_ff2, final_norm_w):
    bsz, seq, d = x.shape
    depth = w_ada.shape[0]
    assert seq % MIXER_TILE == 0 and seq % MLP_TILE == 0
    assert MIXER_TILE % ROW_BLOCK == 0 and MLP_TILE % ROW_BLOCK == 0
    assert ROW_BLOCK % CHUNK == 0
    h = x
    for l in range(depth):
        mod = _ada(c, w_ada[l], b_ada[l]).reshape(bsz, N_MOD, d)
        h = _mixer(h, mod, positions, norm1_w[l], ret_gn_w[l], conv_w[l],
                   w_in[l], w_ret_out[l], w_conv_out[l], w_o[l])
        h = _mlp(h, mod, norm2_w[l], final_norm_w, w_ff1[l], w_ff2[l],
                 final_norm=(l == depth - 1))
    return h
```

```python
import functools

import jax
import jax.numpy as jnp
from jax import lax
from jax.experimental import pallas as pl
from jax.experimental.pallas import tpu as pltpu

CHUNK = 64
RET_HEADS = 4
RET_QK_DIM = 128
RET_V_DIM = 256
CONV_K = 3
N_MOD = 6
ROPE_BASE = 10000.0
EPS = 1e-6

MIXER_TILE = 512
MLP_TILE = 1024
ROW_BLOCK = 256
SUBLANES = 8
BF16_ROWS = 16
STAGE_BYTES = 1024 * 1024
VMEM_LIMIT_BYTES = 56 * 1024 * 1024

_BF16 = jnp.bfloat16
_F32 = jnp.float32


def _const_spec(shape):
    nd = len(shape)
    return pl.BlockSpec(shape, lambda *_: (0,) * nd, pipeline_mode=pl.Buffered(1))


def _dot(a, b):
    return jnp.dot(a, b, preferred_element_type=_F32)


def _stage_shape(w_shape):
    rows, cols = w_shape
    chunk_rows = STAGE_BYTES // (cols * 4)
    assert chunk_rows % BF16_ROWS == 0 and rows % chunk_rows == 0
    return (2, chunk_rows, cols)


def _load_cast(w_hbm, w_vmem, stage, sem):
    chunk_rows = stage.shape[1]
    n_chunks = w_hbm.shape[0] // chunk_rows

    def copy(i, slot):
        return pltpu.make_async_copy(
            w_hbm.at[pl.ds(i * chunk_rows, chunk_rows), :], stage.at[slot], sem.at[slot])

    copy(0, 0).start()

    def body(i, carry):
        slot = lax.rem(i, 2)

        @pl.when(i + 1 < n_chunks)
        def _():
            copy(i + 1, 1 - slot).start()

        copy(i, slot).wait()
        r0 = pl.multiple_of(i * chunk_rows, chunk_rows)
        w_vmem[pl.ds(r0, chunk_rows), :] = stage[slot].astype(_BF16)
        return carry

    lax.fori_loop(0, n_chunks, body, 0)


def _first_step():
    return jnp.logical_and(pl.program_id(0) == 0, pl.program_id(1) == 0)


def _ada_kernel(c_ref, w_ref, b_ref, o_ref):
    o_ref[...] = _dot(jax.nn.silu(c_ref[...]), w_ref[...]) + b_ref[...]


def _ada(c, w_ada, b_ada):
    bsz, d = c.shape
    n = w_ada.shape[1]
    return pl.pallas_call(
        _ada_kernel,
        out_shape=jax.ShapeDtypeStruct((bsz, n), _F32),
        grid=(n // d,),
        in_specs=[
            pl.BlockSpec((bsz, d), lambda j: (0, 0)),
            pl.BlockSpec((d, d), lambda j: (0, j)),
            pl.BlockSpec((1, d), lambda j: (0, j)),
        ],
        out_specs=pl.BlockSpec((bsz, d), lambda j: (0, j)),
        compiler_params=pltpu.CompilerParams(dimension_semantics=("arbitrary",)),
        name="ada",
    )(c, w_ada, b_ada.reshape(1, n))


def _rmsnorm_mod(t, gain, shift):
    ms = jnp.mean(t * t, axis=-1, keepdims=True)
    return t * lax.rsqrt(ms + EPS) * gain + shift


def _mixer_kernel(x_ref, mod_ref, pos_ref, n1w_ref, gnw_ref, convw_ref,
                  invf_ref, sgn_ref, dmask_ref, qdec_ref, kdec_ref, tdec_ref,
                  w_in_hbm, w_ro_hbm, w_co_hbm, w_o_hbm,
                  h_ref,
                  state_ref, zbuf_ref, gated_ref,
                  w_in_ref, w_ro_ref, w_co_ref, w_o_ref,
                  stage_in, stage_sq, sem, *, d_model):
    tt = x_ref.shape[1]
    d = d_model
    qk = RET_HEADS * RET_QK_DIM
    rv = RET_HEADS * RET_V_DIM
    o_q, o_k, o_v = 0, qk, 2 * qk
    o_og = o_v + rv
    o_cb = o_og + rv
    o_cc = o_cb + d
    o_cx = o_cc + d
    o_ga = o_cx + d
    o_gb = o_ga + d

    @pl.when(_first_step())
    def _():
        _load_cast(w_in_hbm, w_in_ref, stage_in, sem)
        _load_cast(w_ro_hbm, w_ro_ref, stage_sq, sem)
        _load_cast(w_co_hbm, w_co_ref, stage_sq, sem)
        _load_cast(w_o_hbm, w_o_ref, stage_sq, sem)

    @pl.when(pl.program_id(1) == 0)
    def _():
        state_ref[...] = jnp.zeros_like(state_ref)
        zbuf_ref[0:SUBLANES, :] = jnp.zeros((SUBLANES, d), _F32)

    mod = mod_ref[0]
    sh1, sc1, g1 = mod[0:1], mod[1:2], mod[2:3]
    gain1 = n1w_ref[...] * (1.0 + sc1)
    kscale = RET_QK_DIM ** -0.5
    half = RET_QK_DIM // 2

    for r0 in range(0, tt, ROW_BLOCK):
        rows = slice(r0, r0 + ROW_BLOCK)
        x = x_ref[0, rows, :]
        ub = _rmsnorm_mod(x, gain1, sh1).astype(_BF16)

        def proj(lo, width):
            return _dot(ub, w_in_ref[:, lo:lo + width])

        ang = pos_ref[0, rows, :].astype(_F32) * invf_ref[...]
        cosf = jnp.cos(ang)
        sinf = jnp.sin(ang) * sgn_ref[...]
        cosk = cosf * kscale
        sink = sinf * kscale

        q = proj(o_q, qk)
        k = proj(o_k, qk)
        v = proj(o_v, rv)
        og = proj(o_og, rv)

        for h in range(RET_HEADS):
            qcols = slice(h * RET_QK_DIM, (h + 1) * RET_QK_DIM)
            vcols = slice(h * RET_V_DIM, (h + 1) * RET_V_DIM)
            qh = q[:, qcols]
            kh = k[:, qcols]
            vh = v[:, vcols].astype(_BF16)
            qr = qh * cosf + pltpu.roll(qh, half, 1) * sinf
            kr = kh * cosk + pltpu.roll(kh, half, 1) * sink
            s = lax.dot_general(qr.astype(_BF16), kr.astype(_BF16),
                                (((1,), (1,)), ((), ())),
                                preferred_element_type=_F32) * dmask_ref[h]
            st = state_ref[h]
            o = _dot(s.astype(_BF16), vh) + _dot(
                (qr * qdec_ref[h]).astype(_BF16), st.astype(_BF16))
            state_ref[h] = tdec_ref[h] * st + lax.dot_general(
                (kr * kdec_ref[h]).astype(_BF16), vh, (((0,), (0,)), ((), ())),
                preferred_element_type=_F32)
            mu = jnp.mean(o, axis=-1, keepdims=True)
            oc = o - mu
            var = jnp.mean(oc * oc, axis=-1, keepdims=True)
            y = oc * lax.rsqrt(var + EPS) * gnw_ref[:, vcols]
            gated_ref[rows, vcols] = (jax.nn.silu(og[:, vcols]) * y).astype(_BF16)
        y_ret = _dot(gated_ref[rows, :], w_ro_ref[...])

        z = proj(o_cc, d) * proj(o_cx, d)
        z0 = SUBLANES + r0
        zbuf_ref[z0:z0 + ROW_BLOCK, :] = z
        z1 = zbuf_ref[z0 - 1:z0 - 1 + ROW_BLOCK, :]
        z2 = zbuf_ref[z0 - 2:z0 - 2 + ROW_BLOCK, :]
        conv = z2 * convw_ref[0:1, :] + z1 * convw_ref[1:2, :] + z * convw_ref[2:3, :]
        y_conv = _dot((proj(o_cb, d) * conv).astype(_BF16), w_co_ref[...])

        merged = (jax.nn.sigmoid(proj(o_ga, d)) * y_ret
                  + jax.nn.sigmoid(proj(o_gb, d)) * y_conv)
        h_ref[0, rows, :] = x + g1 * _dot(merged.astype(_BF16), w_o_ref[...])
    zbuf_ref[0:SUBLANES, :] = zbuf_ref[tt:tt + SUBLANES, :]


def _decay_tables(tile):
    hh = jnp.arange(RET_HEADS, dtype=_F32)
    log_g = jnp.log1p(-jnp.exp2(-5.0 - hh))
    idx = jnp.arange(tile, dtype=_F32)
    diff = idx[:, None] - idx[None, :]
    cn = jnp.arange(tile)[:, None] // CHUNK
    cm = jnp.arange(tile)[None, :] // CHUNK
    expo = jnp.where(cn == cm, jnp.abs(diff), diff)
    dmask = jnp.where((cm <= cn)[None],
                      jnp.exp(expo[None] * log_g[:, None, None]), 0.0)
    qdec = jnp.exp((idx + 1.0)[None, :] * log_g[:, None])
    kdec = jnp.exp((tile - 1.0 - idx)[None, :] * log_g[:, None])
    tdec = jnp.exp(tile * log_g)
    bcast = lambda t: jnp.broadcast_to(t[:, :, None], (RET_HEADS, tile, RET_QK_DIM))
    return dmask, bcast(qdec), bcast(kdec), tdec


def _mixer(x, mod, positions, norm1_w, ret_gn_w, conv_w, w_in, w_ret_out,
           w_conv_out, w_o):
    bsz, seq, d = x.shape
    tt = MIXER_TILE
    rv = RET_HEADS * RET_V_DIM
    half = RET_QK_DIM // 2
    inv_freq = ROPE_BASE ** (-jnp.arange(0, RET_QK_DIM, 2, dtype=_F32) / RET_QK_DIM)
    invf = jnp.concatenate([inv_freq, inv_freq]).reshape(1, RET_QK_DIM)
    sgn = jnp.concatenate([-jnp.ones((half,), _F32),
                           jnp.ones((half,), _F32)]).reshape(1, RET_QK_DIM)
    dmask, qdec, kdec, tdec = _decay_tables(ROW_BLOCK)
    in_width = w_in.shape[1]

    tile_spec = pl.BlockSpec((1, tt, d), lambda b, t: (b, t, 0))
    return pl.pallas_call(
        functools.partial(_mixer_kernel, d_model=d),
        out_shape=jax.ShapeDtypeStruct((bsz, seq, d), _F32),
        grid=(bsz, seq // tt),
        in_specs=[
            tile_spec,
            pl.BlockSpec((1, N_MOD, d), lambda b, t: (b, 0, 0)),
            pl.BlockSpec((1, tt, 1), lambda b, t: (b, t, 0)),
            _const_spec((1, d)),
            _const_spec((1, rv)),
            _const_spec((CONV_K, d)),
            _const_spec((1, RET_QK_DIM)),
            _const_spec((1, RET_QK_DIM)),
            _const_spec((RET_HEADS, ROW_BLOCK, ROW_BLOCK)),
            _const_spec((RET_HEADS, ROW_BLOCK, RET_QK_DIM)),
            _const_spec((RET_HEADS, ROW_BLOCK, RET_QK_DIM)),
            pl.BlockSpec(memory_space=pltpu.SMEM),
            pl.BlockSpec(memory_space=pl.ANY),
            pl.BlockSpec(memory_space=pl.ANY),
            pl.BlockSpec(memory_space=pl.ANY),
            pl.BlockSpec(memory_space=pl.ANY),
        ],
        out_specs=tile_spec,
        scratch_shapes=[
            pltpu.VMEM((RET_HEADS, RET_QK_DIM, RET_V_DIM), _F32),
            pltpu.VMEM((SUBLANES + tt, d), _F32),
            pltpu.VMEM((tt, rv), _BF16),
            pltpu.VMEM((d, in_width), _BF16),
            pltpu.VMEM((rv, d), _BF16),
            pltpu.VMEM((d, d), _BF16),
            pltpu.VMEM((d, d), _BF16),
            pltpu.VMEM(_stage_shape((d, in_width)), _F32),
            pltpu.VMEM(_stage_shape((d, d)), _F32),
            pltpu.SemaphoreType.DMA((2,)),
        ],
        compiler_params=pltpu.CompilerParams(
            dimension_semantics=("arbitrary", "arbitrary"),
            vmem_limit_bytes=VMEM_LIMIT_BYTES),
        name="mixer",
    )(x, mod, positions.reshape(bsz, seq, 1), norm1_w.reshape(1, d),
      ret_gn_w.reshape(1, rv), conv_w, invf, sgn, dmask, qdec, kdec, tdec,
      w_in, w_ret_out, w_conv_out, w_o)


def _mlp_kernel(h_ref, mod_ref, n2w_ref, fw_ref, w1_hbm, w2_hbm, o_ref,
                w1_ref, w2_ref, stage1, stage2, sem, *, final_norm):
    @pl.when(_first_step())
    def _():
        _load_cast(w1_hbm, w1_ref, stage1, sem)
        _load_cast(w2_hbm, w2_ref, stage2, sem)

    mod = mod_ref[0]
    sh2, sc2, g2 = mod[3:4], mod[4:5], mod[5:6]
    gain2 = n2w_ref[...] * (1.0 + sc2)
    for r0 in range(0, h_ref.shape[1], ROW_BLOCK):
        rows = slice(r0, r0 + ROW_BLOCK)
        h = h_ref[0, rows, :]
        u2 = _rmsnorm_mod(h, gain2, sh2).astype(_BF16)
        hid = jnp.square(jnp.maximum(_dot(u2, w1_ref[...]), 0.0)).astype(_BF16)
        h2 = h + g2 * _dot(hid, w2_ref[...])
        if final_norm:
            ms = jnp.mean(h2 * h2, axis=-1, keepdims=True)
            h2 = h2 * lax.rsqrt(ms + EPS) * fw_ref[...]
        o_ref[0, rows, :] = h2


def _mlp(h, mod, norm2_w, final_norm_w, w_ff1, w_ff2, final_norm):
    bsz, seq, d = h.shape
    tm = MLP_TILE
    dff = w_ff1.shape[1]
    tile_spec = pl.BlockSpec((1, tm, d), lambda b, t: (b, t, 0))
    return pl.pallas_call(
        functools.partial(_mlp_kernel, final_norm=final_norm),
        out_shape=jax.ShapeDtypeStruct((bsz, seq, d), _F32),
        grid=(bsz, seq // tm),
        in_specs=[
            tile_spec,
            pl.BlockSpec((1, N_MOD, d), lambda b, t: (b, 0, 0)),
            _const_spec((1, d)),
            _const_spec((1, d)),
            pl.BlockSpec(memory_space=pl.ANY),
            pl.BlockSpec(memory_space=pl.ANY),
        ],
        out_specs=tile_spec,
        scratch_shapes=[
            pltpu.VMEM((d, dff), _BF16),
            pltpu.VMEM((dff, d), _BF16),
            pltpu.VMEM(_stage_shape((d, dff)), _F32),
            pltpu.VMEM(_stage_shape((dff, d)), _F32),
            pltpu.SemaphoreType.DMA((2,)),
        ],
        compiler_params=pltpu.CompilerParams(
            dimension_semantics=("arbitrary", "arbitrary"),
            vmem_limit_bytes=VMEM_LIMIT_BYTES),
        name="mlp",
    )(h, mod, norm2_w.reshape(1, d), final_norm_w.reshape(1, d),
      w_ff1, w_ff2)


def kernel(x, c, positions, w_ada, b_ada, norm1_w, w_in, ret_gn_w, conv_w, w_ret_out, w_conv_out, w_o, norm2_w, w_ff1, w_ff2, final_norm_w):
    bsz, seq, d = x.shape
    depth = w_ada.shape[0]
    assert seq % MIXER_TILE == 0 and seq % MLP_TILE == 0
    assert MIXER_TILE % ROW_BLOCK == 0 and MLP_TILE % ROW_BLOCK == 0
    assert ROW_BLOCK % CHUNK == 0
    h = x
    for l in range(depth):
        mod = _ada(c, w_ada[l], b_ada[l]).reshape(bsz, N_MOD, d)
        h = _mixer(h, mod, positions, norm1_w[l], ret_gn_w[l], conv_w[l],
                   w_in[l], w_ret_out[l], w_conv_out[l], w_o[l])
        h = _mlp(h, mod, norm2_w[l], final_norm_w, w_ff1[l], w_ff2[l],
                 final_norm=(l == depth - 1))
    return h
```

```python
import functools

import jax
import jax.numpy as jnp
from jax import lax
from jax.experimental import pallas as pl
from jax.experimental.pallas import tpu as pltpu

CHUNK = 64
RET_HEADS = 4
RET_QK_DIM = 128
RET_V_DIM = 256
CONV_K = 3
N_MOD = 6
ROPE_BASE = 10000.0
EPS = 1e-6

MIXER_TILE = 512
MLP_TILE = 1024
ROW_BLOCK = 256
SUBLANES = 8
STAGE_ROWS, STAGE_COLS = 512, 1024
STAGE_SLOTS = 4
VMEM_LIMIT_BYTES = 56 * 1024 * 1024

_BF16 = jnp.bfloat16
_F32 = jnp.float32


def _const_spec(shape):
    nd = len(shape)
    return pl.BlockSpec(shape, lambda *_: (0,) * nd, pipeline_mode=pl.Buffered(1))


def _dot(a, b):
    return jnp.dot(a, b, preferred_element_type=_F32)


def _load_cast(pairs, stage, sem):
    chunks = []
    for src, dst in pairs:
        rows, cols = src.shape
        assert rows % STAGE_ROWS == 0 and cols % STAGE_COLS == 0
        for r in range(0, rows, STAGE_ROWS):
            for c in range(0, cols, STAGE_COLS):
                window = (pl.ds(r, STAGE_ROWS), pl.ds(c, STAGE_COLS))
                chunks.append((src.at[window], dst.at[window]))
    copies = [
        pltpu.make_async_copy(src, stage.at[i % STAGE_SLOTS], sem.at[i % STAGE_SLOTS])
        for i, (src, _) in enumerate(chunks)]
    for copy in copies[:STAGE_SLOTS]:
        copy.start()
    for i, (_, dst) in enumerate(chunks):
        copies[i].wait()
        dst[...] = stage[i % STAGE_SLOTS].astype(_BF16)
        if i + STAGE_SLOTS < len(copies):
            copies[i + STAGE_SLOTS].start()


def _first_step():
    return jnp.logical_and(pl.program_id(0) == 0, pl.program_id(1) == 0)


def _ada_kernel(c_ref, w_ref, b_ref, o_ref):
    o_ref[...] = _dot(jax.nn.silu(c_ref[...]), w_ref[...]) + b_ref[...]


def _ada(c, w_ada, b_ada):
    bsz, d = c.shape
    n = w_ada.shape[1]
    return pl.pallas_call(
        _ada_kernel,
        out_shape=jax.ShapeDtypeStruct((bsz, n), _F32),
        grid=(n // d,),
        in_specs=[
            pl.BlockSpec((bsz, d), lambda j: (0, 0)),
            pl.BlockSpec((d, d), lambda j: (0, j)),
            pl.BlockSpec((1, d), lambda j: (0, j)),
        ],
        out_specs=pl.BlockSpec((bsz, d), lambda j: (0, j)),
        compiler_params=pltpu.CompilerParams(dimension_semantics=("arbitrary",)),
        name="ada",
    )(c, w_ada, b_ada.reshape(1, n))


def _rmsnorm_mod(t, gain, shift):
    ms = jnp.mean(t * t, axis=-1, keepdims=True)
    return t * lax.rsqrt(ms + EPS) * gain + shift


def _mixer_kernel(x_ref, mod_ref, pos_ref, n1w_ref, gnw_ref, convw_ref,
                  invf_ref, sgn_ref, dmask_ref, qdec_ref, kdec_ref, tdec_ref,
                  w_in_hbm, w_ro_hbm, w_co_hbm, w_o_hbm,
                  h_ref,
                  state_ref, zbuf_ref, gated_ref,
                  w_in_ref, w_ro_ref, w_co_ref, w_o_ref,
                  stage, sem, *, d_model):
    tt = x_ref.shape[1]
    d = d_model
    qk = RET_HEADS * RET_QK_DIM
    rv = RET_HEADS * RET_V_DIM
    o_q, o_k, o_v = 0, qk, 2 * qk
    o_og = o_v + rv
    o_cb = o_og + rv
    o_cc = o_cb + d
    o_cx = o_cc + d
    o_ga = o_cx + d
    o_gb = o_ga + d

    @pl.when(_first_step())
    def _():
        _load_cast([(w_in_hbm, w_in_ref), (w_ro_hbm, w_ro_ref),
                    (w_co_hbm, w_co_ref), (w_o_hbm, w_o_ref)], stage, sem)

    @pl.when(pl.program_id(1) == 0)
    def _():
        state_ref[...] = jnp.zeros_like(state_ref)
        zbuf_ref[0:SUBLANES, :] = jnp.zeros((SUBLANES, d), _F32)

    mod = mod_ref[0]
    sh1, sc1, g1 = mod[0:1], mod[1:2], mod[2:3]
    gain1 = n1w_ref[...] * (1.0 + sc1)
    kscale = RET_QK_DIM ** -0.5
    half = RET_QK_DIM // 2

    for r0 in range(0, tt, ROW_BLOCK):
        rows = slice(r0, r0 + ROW_BLOCK)
        x = x_ref[0, rows, :]
        ub = _rmsnorm_mod(x, gain1, sh1).astype(_BF16)

        def proj(lo, width):
            return _dot(ub, w_in_ref[:, lo:lo + width])

        ang_t = invf_ref[...] * pos_ref[0, 0, :, rows].astype(_F32)
        cosf = jnp.cos(ang_t).T
        sinf = jnp.sin(ang_t).T * sgn_ref[...]
        cosk = cosf * kscale
        sink = sinf * kscale

        q = proj(o_q, qk)
        k = proj(o_k, qk)
        v = proj(o_v, rv)
        og = proj(o_og, rv)

        for h in range(RET_HEADS):
            qcols = slice(h * RET_QK_DIM, (h + 1) * RET_QK_DIM)
            vcols = slice(h * RET_V_DIM, (h + 1) * RET_V_DIM)
            qh = q[:, qcols]
            kh = k[:, qcols]
            vh = v[:, vcols].astype(_BF16)
            qr = qh * cosf + pltpu.roll(qh, half, 1) * sinf
            kr = kh * cosk + pltpu.roll(kh, half, 1) * sink
            s = lax.dot_general(qr.astype(_BF16), kr.astype(_BF16),
                                (((1,), (1,)), ((), ())),
                                preferred_element_type=_F32) * dmask_ref[h]
            st = state_ref[h]
            o = _dot(s.astype(_BF16), vh) + _dot(
                (qr * qdec_ref[h]).astype(_BF16), st.astype(_BF16))
            state_ref[h] = tdec_ref[h] * st + lax.dot_general(
                (kr * kdec_ref[h]).astype(_BF16), vh, (((0,), (0,)), ((), ())),
                preferred_element_type=_F32)
            mu = jnp.mean(o, axis=-1, keepdims=True)
            oc = o - mu
            var = jnp.mean(oc * oc, axis=-1, keepdims=True)
            y = oc * lax.rsqrt(var + EPS) * gnw_ref[:, vcols]
            gated_ref[rows, vcols] = (jax.nn.silu(og[:, vcols]) * y).astype(_BF16)
        y_ret = _dot(gated_ref[rows, :], w_ro_ref[...])

        z = proj(o_cc, d) * proj(o_cx, d)
        z0 = SUBLANES + r0
        zbuf_ref[z0:z0 + ROW_BLOCK, :] = z
        z1 = zbuf_ref[z0 - 1:z0 - 1 + ROW_BLOCK, :]
        z2 = zbuf_ref[z0 - 2:z0 - 2 + ROW_BLOCK, :]
        conv = z2 * convw_ref[0:1, :] + z1 * convw_ref[1:2, :] + z * convw_ref[2:3, :]
        y_conv = _dot((proj(o_cb, d) * conv).astype(_BF16), w_co_ref[...])

        merged = (jax.nn.sigmoid(proj(o_ga, d)) * y_ret
                  + jax.nn.sigmoid(proj(o_gb, d)) * y_conv)
        h_ref[0, rows, :] = x + g1 * _dot(merged.astype(_BF16), w_o_ref[...])
    zbuf_ref[0:SUBLANES, :] = zbuf_ref[tt:tt + SUBLANES, :]


def _decay_tables(tile):
    hh = jnp.arange(RET_HEADS, dtype=_F32)
    log_g = jnp.log1p(-jnp.exp2(-5.0 - hh))
    idx = jnp.arange(tile, dtype=_F32)
    diff = idx[:, None] - idx[None, :]
    cn = jnp.arange(tile)[:, None] // CHUNK
    cm = jnp.arange(tile)[None, :] // CHUNK
    expo = jnp.where(cn == cm, jnp.abs(diff), diff)
    dmask = jnp.where((cm <= cn)[None],
                      jnp.exp(expo[None] * log_g[:, None, None]), 0.0)
    qdec = jnp.exp((idx + 1.0)[None, :] * log_g[:, None])
    kdec = jnp.exp((tile - 1.0 - idx)[None, :] * log_g[:, None])
    tdec = jnp.exp(tile * log_g)
    bcast = lambda t: jnp.broadcast_to(t[:, :, None], (RET_HEADS, tile, RET_QK_DIM))
    return dmask, bcast(qdec), bcast(kdec), tdec


def _mixer(x, mod, positions, norm1_w, ret_gn_w, conv_w, w_in, w_ret_out,
           w_conv_out, w_o):
    bsz, seq, d = x.shape
    tt = MIXER_TILE
    rv = RET_HEADS * RET_V_DIM
    half = RET_QK_DIM // 2
    inv_freq = ROPE_BASE ** (-jnp.arange(0, RET_QK_DIM, 2, dtype=_F32) / RET_QK_DIM)
    invf = jnp.concatenate([inv_freq, inv_freq]).reshape(RET_QK_DIM, 1)
    sgn = jnp.concatenate([-jnp.ones((half,), _F32),
                           jnp.ones((half,), _F32)]).reshape(1, RET_QK_DIM)
    dmask, qdec, kdec, tdec = _decay_tables(ROW_BLOCK)
    in_width = w_in.shape[1]

    tile_spec = pl.BlockSpec((1, tt, d), lambda b, t: (b, t, 0))
    return pl.pallas_call(
        functools.partial(_mixer_kernel, d_model=d),
        out_shape=jax.ShapeDtypeStruct((bsz, seq, d), _F32),
        grid=(bsz, seq // tt),
        in_specs=[
            tile_spec,
            pl.BlockSpec((1, N_MOD, d), lambda b, t: (b, 0, 0)),
            pl.BlockSpec((1, 1, 1, tt), lambda b, t: (b, t, 0, 0)),
            _const_spec((1, d)),
            _const_spec((1, rv)),
            _const_spec((CONV_K, d)),
            _const_spec((RET_QK_DIM, 1)),
            _const_spec((1, RET_QK_DIM)),
            _const_spec((RET_HEADS, ROW_BLOCK, ROW_BLOCK)),
            _const_spec((RET_HEADS, ROW_BLOCK, RET_QK_DIM)),
            _const_spec((RET_HEADS, ROW_BLOCK, RET_QK_DIM)),
            pl.BlockSpec(memory_space=pltpu.SMEM),
            pl.BlockSpec(memory_space=pl.ANY),
            pl.BlockSpec(memory_space=pl.ANY),
            pl.BlockSpec(memory_space=pl.ANY),
            pl.BlockSpec(memory_space=pl.ANY),
        ],
        out_specs=tile_spec,
        scratch_shapes=[
            pltpu.VMEM((RET_HEADS, RET_QK_DIM, RET_V_DIM), _F32),
            pltpu.VMEM((SUBLANES + tt, d), _F32),
            pltpu.VMEM((tt, rv), _BF16),
            pltpu.VMEM((d, in_width), _BF16),
            pltpu.VMEM((rv, d), _BF16),
            pltpu.VMEM((d, d), _BF16),
            pltpu.VMEM((d, d), _BF16),
            pltpu.VMEM((STAGE_SLOTS, STAGE_ROWS, STAGE_COLS), _F32),
            pltpu.SemaphoreType.DMA((STAGE_SLOTS,)),
        ],
        compiler_params=pltpu.CompilerParams(
            dimension_semantics=("arbitrary", "arbitrary"),
            vmem_limit_bytes=VMEM_LIMIT_BYTES),
        name="mixer",
    )(x, mod, positions.reshape(bsz, seq // tt, 1, tt), norm1_w.reshape(1, d),
      ret_gn_w.reshape(1, rv), conv_w, invf, sgn, dmask, qdec, kdec, tdec,
      w_in, w_ret_out, w_conv_out, w_o)


def _mlp_kernel(h_ref, mod_ref, n2w_ref, fw_ref, w1_hbm, w2_hbm, o_ref,
                w1_ref, w2_ref, stage, sem, *, final_norm):
    @pl.when(_first_step())
    def _():
        _load_cast([(w1_hbm, w1_ref), (w2_hbm, w2_ref)], stage, sem)

    mod = mod_ref[0]
    sh2, sc2, g2 = mod[3:4], mod[4:5], mod[5:6]
    gain2 = n2w_ref[...] * (1.0 + sc2)
    for r0 in range(0, h_ref.shape[1], ROW_BLOCK):
        rows = slice(r0, r0 + ROW_BLOCK)
        h = h_ref[0, rows, :]
        u2 = _rmsnorm_mod(h, gain2, sh2).astype(_BF16)
        hid = jnp.square(jnp.maximum(_dot(u2, w1_ref[...]), 0.0)).astype(_BF16)
        h2 = h + g2 * _dot(hid, w2_ref[...])
        if final_norm:
            ms = jnp.mean(h2 * h2, axis=-1, keepdims=True)
            h2 = h2 * lax.rsqrt(ms + EPS) * fw_ref[...]
        o_ref[0, rows, :] = h2


def _mlp(h, mod, norm2_w, final_norm_w, w_ff1, w_ff2, final_norm):
    bsz, seq, d = h.shape
    tm = MLP_TILE
    dff = w_ff1.shape[1]
    tile_spec = pl.BlockSpec((1, tm, d), lambda b, t: (b, t, 0))
    return pl.pallas_call(
        functools.partial(_mlp_kernel, final_norm=final_norm),
        out_shape=jax.ShapeDtypeStruct((bsz, seq, d), _F32),
        grid=(bsz, seq // tm),
        in_specs=[
            tile_spec,
            pl.BlockSpec((1, N_MOD, d), lambda b, t: (b, 0, 0)),
            _const_spec((1, d)),
            _const_spec((1, d)),
            pl.BlockSpec(memory_space=pl.ANY),
            pl.BlockSpec(memory_space=pl.ANY),
        ],
        out_specs=tile_spec,
        scratch_shapes=[
            pltpu.VMEM((d, dff), _BF16),
            pltpu.VMEM((dff, d), _BF16),
            pltpu.VMEM((STAGE_SLOTS, STAGE_ROWS, STAGE_COLS), _F32),
            pltpu.SemaphoreType.DMA((STAGE_SLOTS,)),
        ],
        compiler_params=pltpu.CompilerParams(
            dimension_semantics=("arbitrary", "arbitrary"),
            vmem_limit_bytes=VMEM_LIMIT_BYTES),
        name="mlp",
    )(h, mod, norm2_w.reshape(1, d), final_norm_w.reshape(1, d),
      w_ff1, w_ff2)


def kernel(x, c, positions, w_ada, b_ada, norm1_w, w_in, ret_gn_w, conv_w, w_ret_out, w_conv_out, w_o, norm2_w, w_ff1, w_ff2, final_norm_w):
    bsz, seq, d = x.shape
    depth = w_ada.shape[0]
    assert seq % MIXER_TILE == 0 and seq % MLP_TILE == 0
    assert MIXER_TILE % ROW_BLOCK == 0 and MLP_TILE % ROW_BLOCK == 0
    assert ROW_BLOCK % CHUNK == 0
    h = x
    for l in range(depth):
        mod = _ada(c, w_ada[l], b_ada[l]).reshape(bsz, N_MOD, d)
        h = _mixer(h, mod, positions, norm1_w[l], ret_gn_w[l], conv_w[l],
                   w_in[l], w_ret_out[l], w_conv_out[l], w_o[l])
        h = _mlp(h, mod, norm2_w[l], final_norm_w, w_ff1[l], w_ff2[l],
                 final_norm=(l == depth - 1))
    return h
```

```python
import functools

import jax
import jax.numpy as jnp
from jax import lax
from jax.experimental import pallas as pl
from jax.experimental.pallas import tpu as pltpu

CHUNK = 64
RET_HEADS = 4
RET_QK_DIM = 128
RET_V_DIM = 256
CONV_K = 3
N_MOD = 6
ROPE_BASE = 10000.0
EPS = 1e-6

MIXER_TILE = 512
MLP_TILE = 1024
ROW_BLOCK = 256
SUBLANES = 8
STAGE_ROWS, STAGE_COLS = 512, 1024
STAGE_SLOTS = 4
VMEM_LIMIT_BYTES = 56 * 1024 * 1024

_BF16 = jnp.bfloat16
_F32 = jnp.float32


def _const_spec(shape):
    nd = len(shape)
    return pl.BlockSpec(shape, lambda *_: (0,) * nd, pipeline_mode=pl.Buffered(1))


def _dot(a, b):
    return jnp.dot(a, b, preferred_element_type=_F32)


def _load_cast(pairs, stage, sem, on_chunk=None):
    chunks = []
    for n, (src, dst) in enumerate(pairs):
        rows, cols = src.shape
        assert rows % STAGE_ROWS == 0 and cols % STAGE_COLS == 0
        for r in range(0, rows, STAGE_ROWS):
            for c in range(0, cols, STAGE_COLS):
                window = (pl.ds(r, STAGE_ROWS), pl.ds(c, STAGE_COLS))
                chunks.append((src.at[window], dst.at[window], (n, r, c)))
    copies = [
        pltpu.make_async_copy(src, stage.at[i % STAGE_SLOTS], sem.at[i % STAGE_SLOTS])
        for i, (src, _, _) in enumerate(chunks)]
    for copy in copies[:STAGE_SLOTS]:
        copy.start()
    for i, (_, dst, where) in enumerate(chunks):
        copies[i].wait()
        chunk = stage[i % STAGE_SLOTS]
        dst[...] = chunk.astype(_BF16)
        if on_chunk is not None:
            on_chunk(*where, chunk)
        if i + STAGE_SLOTS < len(copies):
            copies[i + STAGE_SLOTS].start()


def _first_step():
    return jnp.logical_and(pl.program_id(0) == 0, pl.program_id(1) == 0)


def _ada_kernel(c_ref, w_ref, b_ref, o_ref):
    o_ref[...] = _dot(jax.nn.silu(c_ref[...]), w_ref[...]) + b_ref[...]


def _ada(c, w_ada, b_ada):
    bsz, d = c.shape
    n = w_ada.shape[1]
    return pl.pallas_call(
        _ada_kernel,
        out_shape=jax.ShapeDtypeStruct((bsz, n), _F32),
        grid=(n // d,),
        in_specs=[
            pl.BlockSpec((bsz, d), lambda j: (0, 0)),
            pl.BlockSpec((d, d), lambda j: (0, j)),
            pl.BlockSpec((1, d), lambda j: (0, j)),
        ],
        out_specs=pl.BlockSpec((bsz, d), lambda j: (0, j)),
        compiler_params=pltpu.CompilerParams(dimension_semantics=("arbitrary",)),
        name="ada",
    )(c, w_ada, b_ada.reshape(1, n))


def _rmsnorm_mod(t, gain, shift):
    ms = jnp.mean(t * t, axis=-1, keepdims=True)
    return t * lax.rsqrt(ms + EPS) * gain + shift


def _mixer_kernel(x_ref, mod_ref, pos_ref, n1w_ref, gnw_ref, convw_ref,
                  invf_ref, dmask_ref, qdec_ref, kdec_ref, tdec_ref,
                  w_in_hbm, w_ro_hbm, w_co_hbm, w_o_hbm,
                  h_ref,
                  state_ref, zbuf_ref, gated_ref,
                  w_in_ref, w_ro_ref, w_co_ref, w_o_ref, w_kt_ref,
                  stage, sem, *, d_model):
    tt = x_ref.shape[1]
    d = d_model
    qk = RET_HEADS * RET_QK_DIM
    rv = RET_HEADS * RET_V_DIM
    o_q, o_k, o_v = 0, qk, 2 * qk
    o_og = o_v + rv
    o_cb = o_og + rv
    o_cc = o_cb + d
    o_cx = o_cc + d
    o_ga = o_cx + d
    o_gb = o_ga + d

    @pl.when(_first_step())
    def _():
        def keep_k_transposed(src_index, r, c, chunk):
            if src_index == 0 and c <= o_k < c + STAGE_COLS:
                kcols = chunk[:, o_k - c:o_k - c + qk]
                w_kt_ref[:, r:r + STAGE_ROWS] = kcols.T.astype(_BF16)

        _load_cast([(w_in_hbm, w_in_ref), (w_ro_hbm, w_ro_ref),
                    (w_co_hbm, w_co_ref), (w_o_hbm, w_o_ref)], stage, sem,
                   on_chunk=keep_k_transposed)

    @pl.when(pl.program_id(1) == 0)
    def _():
        state_ref[...] = jnp.zeros_like(state_ref)
        zbuf_ref[0:SUBLANES, :] = jnp.zeros((SUBLANES, d), _F32)

    mod = mod_ref[0]
    sh1, sc1, g1 = mod[0:1], mod[1:2], mod[2:3]
    gain1 = n1w_ref[...] * (1.0 + sc1)
    kscale = RET_QK_DIM ** -0.5
    half = RET_QK_DIM // 2

    for r0 in range(0, tt, ROW_BLOCK):
        rows = slice(r0, r0 + ROW_BLOCK)
        x = x_ref[0, rows, :]
        ub = _rmsnorm_mod(x, gain1, sh1).astype(_BF16)

        def proj(lo, width):
            return _dot(ub, w_in_ref[:, lo:lo + width])

        ang_t = invf_ref[...] * pos_ref[0, 0, :, rows].astype(_F32)
        cos_h = jnp.cos(ang_t)
        sin_h = jnp.sin(ang_t)
        cos_t = jnp.concatenate([cos_h, cos_h], axis=0)
        sin_t = jnp.concatenate([-sin_h, sin_h], axis=0)
        cosf = cos_t.T
        sinf = sin_t.T
        cosk_t = cos_t * kscale
        sink_t = sin_t * kscale

        q = proj(o_q, qk)
        k_t = lax.dot_general(w_kt_ref[...], ub, (((1,), (1,)), ((), ())),
                              preferred_element_type=_F32)
        v = proj(o_v, rv)
        og = proj(o_og, rv)

        for h in range(RET_HEADS):
            qcols = slice(h * RET_QK_DIM, (h + 1) * RET_QK_DIM)
            vcols = slice(h * RET_V_DIM, (h + 1) * RET_V_DIM)
            qh = q[:, qcols]
            kh_t = k_t[qcols, :]
            kh_swap = jnp.concatenate([kh_t[half:], kh_t[:half]], axis=0)
            vh = v[:, vcols].astype(_BF16)
            qr = qh * cosf + pltpu.roll(qh, half, 1) * sinf
            kr_t = kh_t * cosk_t + kh_swap * sink_t
            s = _dot(qr.astype(_BF16), kr_t.astype(_BF16)) * dmask_ref[h]
            st = state_ref[h]
            o = _dot(
                jnp.concatenate([s.astype(_BF16), (qr * qdec_ref[h]).astype(_BF16)], axis=1),
                jnp.concatenate([vh, st.astype(_BF16)], axis=0))
            state_ref[h] = tdec_ref[h] * st + _dot(
                (kr_t * kdec_ref[h]).astype(_BF16), vh)
            mu = jnp.mean(o, axis=-1, keepdims=True)
            oc = o - mu
            var = jnp.mean(oc * oc, axis=-1, keepdims=True)
            y = oc * lax.rsqrt(var + EPS) * gnw_ref[:, vcols]
            gated_ref[rows, vcols] = (jax.nn.silu(og[:, vcols]) * y).astype(_BF16)
        y_ret = _dot(gated_ref[rows, :], w_ro_ref[...])

        z = proj(o_cc, d) * proj(o_cx, d)
        z0 = SUBLANES + r0
        zbuf_ref[z0:z0 + ROW_BLOCK, :] = z
        z1 = zbuf_ref[z0 - 1:z0 - 1 + ROW_BLOCK, :]
        z2 = zbuf_ref[z0 - 2:z0 - 2 + ROW_BLOCK, :]
        conv = z2 * convw_ref[0:1, :] + z1 * convw_ref[1:2, :] + z * convw_ref[2:3, :]
        y_conv = _dot((proj(o_cb, d) * conv).astype(_BF16), w_co_ref[...])

        merged = (jax.nn.sigmoid(proj(o_ga, d)) * y_ret
                  + jax.nn.sigmoid(proj(o_gb, d)) * y_conv)
        h_ref[0, rows, :] = x + g1 * _dot(merged.astype(_BF16), w_o_ref[...])
    zbuf_ref[0:SUBLANES, :] = zbuf_ref[tt:tt + SUBLANES, :]


def _decay_tables(tile):
    hh = jnp.arange(RET_HEADS, dtype=_F32)
    log_g = jnp.log1p(-jnp.exp2(-5.0 - hh))
    idx = jnp.arange(tile, dtype=_F32)
    diff = idx[:, None] - idx[None, :]
    cn = jnp.arange(tile)[:, None] // CHUNK
    cm = jnp.arange(tile)[None, :] // CHUNK
    expo = jnp.where(cn == cm, jnp.abs(diff), diff)
    dmask = jnp.where((cm <= cn)[None],
                      jnp.exp(expo[None] * log_g[:, None, None]), 0.0)
    qdec = jnp.exp((idx + 1.0)[None, :] * log_g[:, None])
    kdec = jnp.exp((tile - 1.0 - idx)[None, :] * log_g[:, None])
    tdec = jnp.exp(tile * log_g)
    qdec = jnp.broadcast_to(qdec[:, :, None], (RET_HEADS, tile, RET_QK_DIM))
    return dmask, qdec, kdec[:, None, :], tdec


def _mixer(x, mod, positions, norm1_w, ret_gn_w, conv_w, w_in, w_ret_out,
           w_conv_out, w_o):
    bsz, seq, d = x.shape
    tt = MIXER_TILE
    rv = RET_HEADS * RET_V_DIM
    qk = RET_HEADS * RET_QK_DIM
    half = RET_QK_DIM // 2
    inv_freq = ROPE_BASE ** (-jnp.arange(0, RET_QK_DIM, 2, dtype=_F32) / RET_QK_DIM)
    invf = inv_freq.reshape(half, 1)
    dmask, qdec, kdec, tdec = _decay_tables(ROW_BLOCK)
    in_width = w_in.shape[1]

    tile_spec = pl.BlockSpec((1, tt, d), lambda b, t: (b, t, 0))
    return pl.pallas_call(
        functools.partial(_mixer_kernel, d_model=d),
        out_shape=jax.ShapeDtypeStruct((bsz, seq, d), _F32),
        grid=(bsz, seq // tt),
        in_specs=[
            tile_spec,
            pl.BlockSpec((1, N_MOD, d), lambda b, t: (b, 0, 0)),
            pl.BlockSpec((1, 1, 1, tt), lambda b, t: (b, t, 0, 0)),
            _const_spec((1, d)),
            _const_spec((1, rv)),
            _const_spec((CONV_K, d)),
            _const_spec((half, 1)),
            _const_spec((RET_HEADS, ROW_BLOCK, ROW_BLOCK)),
            _const_spec((RET_HEADS, ROW_BLOCK, RET_QK_DIM)),
            _const_spec((RET_HEADS, 1, ROW_BLOCK)),
            pl.BlockSpec(memory_space=pltpu.SMEM),
            pl.BlockSpec(memory_space=pl.ANY),
            pl.BlockSpec(memory_space=pl.ANY),
            pl.BlockSpec(memory_space=pl.ANY),
            pl.BlockSpec(memory_space=pl.ANY),
        ],
        out_specs=tile_spec,
        scratch_shapes=[
            pltpu.VMEM((RET_HEADS, RET_QK_DIM, RET_V_DIM), _F32),
            pltpu.VMEM((SUBLANES + tt, d), _F32),
            pltpu.VMEM((tt, rv), _BF16),
            pltpu.VMEM((d, in_width), _BF16),
            pltpu.VMEM((rv, d), _BF16),
            pltpu.VMEM((d, d), _BF16),
            pltpu.VMEM((d, d), _BF16),
            pltpu.VMEM((qk, d), _BF16),
            pltpu.VMEM((STAGE_SLOTS, STAGE_ROWS, STAGE_COLS), _F32),
            pltpu.SemaphoreType.DMA((STAGE_SLOTS,)),
        ],
        compiler_params=pltpu.CompilerParams(
            dimension_semantics=("arbitrary", "arbitrary"),
            vmem_limit_bytes=VMEM_LIMIT_BYTES),
        name="mixer",
    )(x, mod, positions.reshape(bsz, seq // tt, 1, tt), norm1_w.reshape(1, d),
      ret_gn_w.reshape(1, rv), conv_w, invf, dmask, qdec, kdec, tdec,
      w_in, w_ret_out, w_conv_out, w_o)


def _mlp_kernel(h_ref, mod_ref, n2w_ref, fw_ref, w1_hbm, w2_hbm, o_ref,
                w1_ref, w2_ref, stage, sem, *, final_norm):
    @pl.when(_first_step())
    def _():
        _load_cast([(w1_hbm, w1_ref), (w2_hbm, w2_ref)], stage, sem)

    mod = mod_ref[0]
    sh2, sc2, g2 = mod[3:4], mod[4:5], mod[5:6]
    gain2 = n2w_ref[...] * (1.0 + sc2)
    for r0 in range(0, h_ref.shape[1], ROW_BLOCK):
        rows = slice(r0, r0 + ROW_BLOCK)
        h = h_ref[0, rows, :]
        u2 = _rmsnorm_mod(h, gain2, sh2).astype(_BF16)
        hid = jnp.square(jnp.maximum(_dot(u2, w1_ref[...]), 0.0)).astype(_BF16)
        h2 = h + g2 * _dot(hid, w2_ref[...])
        if final_norm:
            ms = jnp.mean(h2 * h2, axis=-1, keepdims=True)
            h2 = h2 * lax.rsqrt(ms + EPS) * fw_ref[...]
        o_ref[0, rows, :] = h2


def _mlp(h, mod, norm2_w, final_norm_w, w_ff1, w_ff2, final_norm):
    bsz, seq, d = h.shape
    tm = MLP_TILE
    dff = w_ff1.shape[1]
    tile_spec = pl.BlockSpec((1, tm, d), lambda b, t: (b, t, 0))
    return pl.pallas_call(
        functools.partial(_mlp_kernel, final_norm=final_norm),
        out_shape=jax.ShapeDtypeStruct((bsz, seq, d), _F32),
        grid=(bsz, seq // tm),
        in_specs=[
            tile_spec,
            pl.BlockSpec((1, N_MOD, d), lambda b, t: (b, 0, 0)),
            _const_spec((1, d)),
            _const_spec((1, d)),
            pl.BlockSpec(memory_space=pl.ANY),
            pl.BlockSpec(memory_space=pl.ANY),
        ],
        out_specs=tile_spec,
        scratch_shapes=[
            pltpu.VMEM((d, dff), _BF16),
            pltpu.VMEM((dff, d), _BF16),
            pltpu.VMEM((STAGE_SLOTS, STAGE_ROWS, STAGE_COLS), _F32),
            pltpu.SemaphoreType.DMA((STAGE_SLOTS,)),
        ],
        compiler_params=pltpu.CompilerParams(
            dimension_semantics=("arbitrary", "arbitrary"),
            vmem_limit_bytes=VMEM_LIMIT_BYTES),
        name="mlp",
    )(h, mod, norm2_w.reshape(1, d), final_norm_w.reshape(1, d),
      w_ff1, w_ff2)


def kernel(x, c, positions, w_ada, b_ada, norm1_w, w_in, ret_gn_w, conv_w, w_ret_out, w_conv_out, w_o, norm2_w, w_ff1, w_ff2, final_norm_w):
    bsz, seq, d = x.shape
    depth = w_ada.shape[0]
    assert seq % MIXER_TILE == 0 and seq % MLP_TILE == 0
    assert MIXER_TILE % ROW_BLOCK == 0 and MLP_TILE % ROW_BLOCK == 0
    assert ROW_BLOCK % CHUNK == 0
    h = x
    for l in range(depth):
        mod = _ada(c, w_ada[l], b_ada[l]).reshape(bsz, N_MOD, d)
        h = _mixer(h, mod, positions, norm1_w[l], ret_gn_w[l], conv_w[l],
                   w_in[l], w_ret_out[l], w_conv_out[l], w_o[l])
        h = _mlp(h, mod, norm2_w[l], final_norm_w, w_ff1[l], w_ff2[l],
                 final_norm=(l == depth - 1))
    return h
```

```python
import functools

import jax
import jax.numpy as jnp
from jax import lax
from jax.experimental import pallas as pl
from jax.experimental.pallas import tpu as pltpu

CHUNK = 64
RET_HEADS = 4
RET_QK_DIM = 128
RET_V_DIM = 256
CONV_K = 3
N_MOD = 6
ROPE_BASE = 10000.0
EPS = 1e-6

MIXER_TILE = 512
MLP_TILE = 1024
ROW_BLOCK = 256
SUBLANES = 8
STAGE_ROWS, STAGE_COLS = 512, 1024
STAGE_SLOTS = 4
VMEM_LIMIT_BYTES = 56 * 1024 * 1024

_BF16 = jnp.bfloat16
_F32 = jnp.float32


def _const_spec(shape):
    nd = len(shape)
    return pl.BlockSpec(shape, lambda *_: (0,) * nd, pipeline_mode=pl.Buffered(1))


def _dot(a, b):
    return jnp.dot(a, b, preferred_element_type=_F32)


def _load_cast(pairs, stage, sem, on_chunk=None):
    chunks = []
    for n, (src, dst) in enumerate(pairs):
        rows, cols = src.shape
        assert rows % STAGE_ROWS == 0 and cols % STAGE_COLS == 0
        for r in range(0, rows, STAGE_ROWS):
            for c in range(0, cols, STAGE_COLS):
                window = (pl.ds(r, STAGE_ROWS), pl.ds(c, STAGE_COLS))
                chunks.append((src.at[window], dst.at[window], (n, r, c)))
    copies = [
        pltpu.make_async_copy(src, stage.at[i % STAGE_SLOTS], sem.at[i % STAGE_SLOTS])
        for i, (src, _, _) in enumerate(chunks)]
    for copy in copies[:STAGE_SLOTS]:
        copy.start()
    for i, (_, dst, where) in enumerate(chunks):
        copies[i].wait()
        chunk = stage[i % STAGE_SLOTS]
        dst[...] = chunk.astype(_BF16)
        if on_chunk is not None:
            on_chunk(*where, chunk)
        if i + STAGE_SLOTS < len(copies):
            copies[i + STAGE_SLOTS].start()


def _first_step():
    return jnp.logical_and(pl.program_id(0) == 0, pl.program_id(1) == 0)


def _ada_kernel(c_ref, w_ref, b_ref, o_ref):
    o_ref[...] = _dot(jax.nn.silu(c_ref[...]), w_ref[...]) + b_ref[...]


def _ada(c, w_ada, b_ada):
    bsz, d = c.shape
    n = w_ada.shape[1]
    return pl.pallas_call(
        _ada_kernel,
        out_shape=jax.ShapeDtypeStruct((bsz, n), _F32),
        grid=(n // d,),
        in_specs=[
            pl.BlockSpec((bsz, d), lambda j: (0, 0)),
            pl.BlockSpec((d, d), lambda j: (0, j)),
            pl.BlockSpec((1, d), lambda j: (0, j)),
        ],
        out_specs=pl.BlockSpec((bsz, d), lambda j: (0, j)),
        compiler_params=pltpu.CompilerParams(dimension_semantics=("arbitrary",)),
        name="ada",
    )(c, w_ada, b_ada.reshape(1, n))


def _rmsnorm_mod(t, gain, shift):
    ms = jnp.mean(t * t, axis=-1, keepdims=True)
    return t * lax.rsqrt(ms + EPS) * gain + shift


def _mixer_kernel(x_ref, mod_ref, pos_ref, n1w_ref, gnw_ref, convw_ref,
                  invf_ref, dmask_ref, qdec_ref, kdec_ref, tdec_ref,
                  w_in_hbm, w_ro_hbm, w_co_hbm, w_o_hbm,
                  h_ref,
                  state_ref, zbuf_ref, gated_ref, cbuf_ref,
                  w_in_ref, w_ro_ref, w_co_ref, w_o_ref, w_kt_ref,
                  stage, sem, *, d_model):
    tt = x_ref.shape[1]
    d = d_model
    qk = RET_HEADS * RET_QK_DIM
    rv = RET_HEADS * RET_V_DIM
    o_q, o_k, o_v = 0, qk, 2 * qk
    o_og = o_v + rv
    o_cb = o_og + rv
    o_cc = o_cb + d
    o_cx = o_cc + d
    o_ga = o_cx + d
    o_gb = o_ga + d

    @pl.when(_first_step())
    def _():
        def keep_k_transposed(src_index, r, c, chunk):
            if src_index == 0 and c <= o_k < c + STAGE_COLS:
                kcols = chunk[:, o_k - c:o_k - c + qk]
                w_kt_ref[:, r:r + STAGE_ROWS] = kcols.T.astype(_BF16)

        _load_cast([(w_in_hbm, w_in_ref), (w_ro_hbm, w_ro_ref),
                    (w_co_hbm, w_co_ref), (w_o_hbm, w_o_ref)], stage, sem,
                   on_chunk=keep_k_transposed)

    @pl.when(pl.program_id(1) == 0)
    def _():
        state_ref[...] = jnp.zeros_like(state_ref)
        zbuf_ref[0:SUBLANES, :] = jnp.zeros((SUBLANES, d), _F32)

    mod = mod_ref[0]
    sh1, sc1, g1 = mod[0:1], mod[1:2], mod[2:3]
    gain1 = n1w_ref[...] * (1.0 + sc1)
    kscale = RET_QK_DIM ** -0.5
    half = RET_QK_DIM // 2
    cw = d // RET_HEADS

    for r0 in range(0, tt, ROW_BLOCK):
        rows = slice(r0, r0 + ROW_BLOCK)
        x = x_ref[0, rows, :]
        ub = _rmsnorm_mod(x, gain1, sh1).astype(_BF16)

        def proj(lo, width):
            return _dot(ub, w_in_ref[:, lo:lo + width])

        ang_t = invf_ref[...] * pos_ref[0, 0, :, rows].astype(_F32)
        cos_h = jnp.cos(ang_t)
        sin_h = jnp.sin(ang_t)
        cos_t = jnp.concatenate([cos_h, cos_h], axis=0)
        sin_t = jnp.concatenate([-sin_h, sin_h], axis=0)
        cosf = cos_t.T
        sinf = sin_t.T
        cosk_t = cos_t * kscale
        sink_t = sin_t * kscale

        q = proj(o_q, qk)
        k_t = lax.dot_general(w_kt_ref[...], ub, (((1,), (1,)), ((), ())),
                              preferred_element_type=_F32)
        v = proj(o_v, rv)
        og = proj(o_og, rv)

        z0 = SUBLANES + r0
        sig_a, sig_b = [], []
        for h in range(RET_HEADS):
            qcols = slice(h * RET_QK_DIM, (h + 1) * RET_QK_DIM)
            vcols = slice(h * RET_V_DIM, (h + 1) * RET_V_DIM)
            ccols = slice(h * cw, (h + 1) * cw)
            qh = q[:, qcols]
            kh_t = k_t[qcols, :]
            kh_swap = jnp.concatenate([kh_t[half:], kh_t[:half]], axis=0)
            vh = v[:, vcols].astype(_BF16)
            qr = qh * cosf + pltpu.roll(qh, half, 1) * sinf
            kr_t = kh_t * cosk_t + kh_swap * sink_t
            s = _dot(qr.astype(_BF16), kr_t.astype(_BF16)) * dmask_ref[h]
            z = proj(o_cc + h * cw, cw) * proj(o_cx + h * cw, cw)
            zbuf_ref[z0:z0 + ROW_BLOCK, ccols] = z
            st = state_ref[h]
            o = _dot(
                jnp.concatenate([s.astype(_BF16), (qr * qdec_ref[h]).astype(_BF16)], axis=1),
                jnp.concatenate([vh, st.astype(_BF16)], axis=0))
            cb = proj(o_cb + h * cw, cw)
            sig_a.append(jax.nn.sigmoid(proj(o_ga + h * cw, cw)))
            state_ref[h] = tdec_ref[h] * st + _dot(
                (kr_t * kdec_ref[h]).astype(_BF16), vh)
            sig_b.append(jax.nn.sigmoid(proj(o_gb + h * cw, cw)))
            mu = jnp.mean(o, axis=-1, keepdims=True)
            oc = o - mu
            var = jnp.mean(oc * oc, axis=-1, keepdims=True)
            y = oc * lax.rsqrt(var + EPS) * gnw_ref[:, vcols]
            gated_ref[rows, vcols] = (jax.nn.silu(og[:, vcols]) * y).astype(_BF16)
            z1 = zbuf_ref[z0 - 1:z0 - 1 + ROW_BLOCK, ccols]
            z2 = zbuf_ref[z0 - 2:z0 - 2 + ROW_BLOCK, ccols]
            conv = (z2 * convw_ref[0:1, ccols] + z1 * convw_ref[1:2, ccols]
                    + z * convw_ref[2:3, ccols])
            cbuf_ref[rows, ccols] = (cb * conv).astype(_BF16)
        y_ret = _dot(gated_ref[rows, :], w_ro_ref[...])
        y_conv = _dot(cbuf_ref[rows, :], w_co_ref[...])
        merged = (jnp.concatenate(sig_a, axis=1) * y_ret
                  + jnp.concatenate(sig_b, axis=1) * y_conv)
        h_ref[0, rows, :] = x + g1 * _dot(merged.astype(_BF16), w_o_ref[...])
    zbuf_ref[0:SUBLANES, :] = zbuf_ref[tt:tt + SUBLANES, :]


def _decay_tables(tile):
    hh = jnp.arange(RET_HEADS, dtype=_F32)
    log_g = jnp.log1p(-jnp.exp2(-5.0 - hh))
    idx = jnp.arange(tile, dtype=_F32)
    diff = idx[:, None] - idx[None, :]
    cn = jnp.arange(tile)[:, None] // CHUNK
    cm = jnp.arange(tile)[None, :] // CHUNK
    expo = jnp.where(cn == cm, jnp.abs(diff), diff)
    dmask = jnp.where((cm <= cn)[None],
                      jnp.exp(expo[None] * log_g[:, None, None]), 0.0)
    qdec = jnp.exp((idx + 1.0)[None, :] * log_g[:, None])
    kdec = jnp.exp((tile - 1.0 - idx)[None, :] * log_g[:, None])
    tdec = jnp.exp(tile * log_g)
    qdec = jnp.broadcast_to(qdec[:, :, None], (RET_HEADS, tile, RET_QK_DIM))
    return dmask, qdec, kdec[:, None, :], tdec


def _mixer(x, mod, positions, norm1_w, ret_gn_w, conv_w, w_in, w_ret_out,
           w_conv_out, w_o):
    bsz, seq, d = x.shape
    tt = MIXER_TILE
    rv = RET_HEADS * RET_V_DIM
    qk = RET_HEADS * RET_QK_DIM
    half = RET_QK_DIM // 2
    inv_freq = ROPE_BASE ** (-jnp.arange(0, RET_QK_DIM, 2, dtype=_F32) / RET_QK_DIM)
    invf = inv_freq.reshape(half, 1)
    dmask, qdec, kdec, tdec = _decay_tables(ROW_BLOCK)
    in_width = w_in.shape[1]

    tile_spec = pl.BlockSpec((1, tt, d), lambda b, t: (b, t, 0))
    return pl.pallas_call(
        functools.partial(_mixer_kernel, d_model=d),
        out_shape=jax.ShapeDtypeStruct((bsz, seq, d), _F32),
        grid=(bsz, seq // tt),
        in_specs=[
            tile_spec,
            pl.BlockSpec((1, N_MOD, d), lambda b, t: (b, 0, 0)),
            pl.BlockSpec((1, 1, 1, tt), lambda b, t: (b, t, 0, 0)),
            _const_spec((1, d)),
            _const_spec((1, rv)),
            _const_spec((CONV_K, d)),
            _const_spec((half, 1)),
            _const_spec((RET_HEADS, ROW_BLOCK, ROW_BLOCK)),
            _const_spec((RET_HEADS, ROW_BLOCK, RET_QK_DIM)),
            _const_spec((RET_HEADS, 1, ROW_BLOCK)),
            pl.BlockSpec(memory_space=pltpu.SMEM),
            pl.BlockSpec(memory_space=pl.ANY),
            pl.BlockSpec(memory_space=pl.ANY),
            pl.BlockSpec(memory_space=pl.ANY),
            pl.BlockSpec(memory_space=pl.ANY),
        ],
        out_specs=tile_spec,
        scratch_shapes=[
            pltpu.VMEM((RET_HEADS, RET_QK_DIM, RET_V_DIM), _F32),
            pltpu.VMEM((SUBLANES + tt, d), _F32),
            pltpu.VMEM((tt, rv), _BF16),
            pltpu.VMEM((tt, d), _BF16),
            pltpu.VMEM((d, in_width), _BF16),
            pltpu.VMEM((rv, d), _BF16),
            pltpu.VMEM((d, d), _BF16),
            pltpu.VMEM((d, d), _BF16),
            pltpu.VMEM((qk, d), _BF16),
            pltpu.VMEM((STAGE_SLOTS, STAGE_ROWS, STAGE_COLS), _F32),
            pltpu.SemaphoreType.DMA((STAGE_SLOTS,)),
        ],
        compiler_params=pltpu.CompilerParams(
            dimension_semantics=("arbitrary", "arbitrary"),
            vmem_limit_bytes=VMEM_LIMIT_BYTES),
        name="mixer",
    )(x, mod, positions.reshape(bsz, seq // tt, 1, tt), norm1_w.reshape(1, d),
      ret_gn_w.reshape(1, rv), conv_w, invf, dmask, qdec, kdec, tdec,
      w_in, w_ret_out, w_conv_out, w_o)


def _mlp_kernel(h_ref, mod_ref, n2w_ref, fw_ref, w1_hbm, w2_hbm, o_ref,
                w1_ref, w2_ref, stage, sem, *, final_norm):
    @pl.when(_first_step())
    def _():
        _load_cast([(w1_hbm, w1_ref), (w2_hbm, w2_ref)], stage, sem)

    mod = mod_ref[0]
    sh2, sc2, g2 = mod[3:4], mod[4:5], mod[5:6]
    gain2 = n2w_ref[...] * (1.0 + sc2)
    for r0 in range(0, h_ref.shape[1], ROW_BLOCK):
        rows = slice(r0, r0 + ROW_BLOCK)
        h = h_ref[0, rows, :]
        u2 = _rmsnorm_mod(h, gain2, sh2).astype(_BF16)
        hid = jnp.square(jnp.maximum(_dot(u2, w1_ref[...]), 0.0)).astype(_BF16)
        h2 = h + g2 * _dot(hid, w2_ref[...])
        if final_norm:
            ms = jnp.mean(h2 * h2, axis=-1, keepdims=True)
            h2 = h2 * lax.rsqrt(ms + EPS) * fw_ref[...]
        o_ref[0, rows, :] = h2


def _mlp(h, mod, norm2_w, final_norm_w, w_ff1, w_ff2, final_norm):
    bsz, seq, d = h.shape
    tm = MLP_TILE
    dff = w_ff1.shape[1]
    tile_spec = pl.BlockSpec((1, tm, d), lambda b, t: (b, t, 0))
    return pl.pallas_call(
        functools.partial(_mlp_kernel, final_norm=final_norm),
        out_shape=jax.ShapeDtypeStruct((bsz, seq, d), _F32),
        grid=(bsz, seq // tm),
        in_specs=[
            tile_spec,
            pl.BlockSpec((1, N_MOD, d), lambda b, t: (b, 0, 0)),
            _const_spec((1, d)),
            _const_spec((1, d)),
            pl.BlockSpec(memory_space=pl.ANY),
            pl.BlockSpec(memory_space=pl.ANY),
        ],
        out_specs=tile_spec,
        scratch_shapes=[
            pltpu.VMEM((d, dff), _BF16),
            pltpu.VMEM((dff, d), _BF16),
            pltpu.VMEM((STAGE_SLOTS, STAGE_ROWS, STAGE_COLS), _F32),
            pltpu.SemaphoreType.DMA((STAGE_SLOTS,)),
        ],
        compiler_params=pltpu.CompilerParams(
            dimension_semantics=("arbitrary", "arbitrary"),
            vmem_limit_bytes=VMEM_LIMIT_BYTES),
        name="mlp",
    )(h, mod, norm2_w.reshape(1, d), final_norm_w.reshape(1, d),
      w_ff1, w_ff2)


def kernel(x, c, positions, w_ada, b_ada, norm1_w, w_in, ret_gn_w, conv_w, w_ret_out, w_conv_out, w_o, norm2_w, w_ff1, w_ff2, final_norm_w):
    bsz, seq, d = x.shape
    depth = w_ada.shape[0]
    assert seq % MIXER_TILE == 0 and seq % MLP_TILE == 0
    assert MIXER_TILE % ROW_BLOCK == 0 and MLP_TILE % ROW_BLOCK == 0
    assert ROW_BLOCK % CHUNK == 0
    h = x
    for l in range(depth):
        mod = _ada(c, w_ada[l], b_ada[l]).reshape(bsz, N_MOD, d)
        h = _mixer(h, mod, positions, norm1_w[l], ret_gn_w[l], conv_w[l],
                   w_in[l], w_ret_out[l], w_conv_out[l], w_o[l])
        h = _mlp(h, mod, norm2_w[l], final_norm_w, w_ff1[l], w_ff2[l],
                 final_norm=(l == depth - 1))
    return h
```

```python
import functools
import math

import jax
import jax.numpy as jnp
from jax import lax
from jax.experimental import pallas as pl
from jax.experimental.pallas import tpu as pltpu

CHUNK = 64
CHUNK_SHIFT = 6
RET_HEADS = 4
RET_QK_DIM = 128
RET_V_DIM = 256
CONV_K = 3
N_MOD = 6
ROPE_BASE = 10000.0
EPS = 1e-6

MIXER_TILE = 512
MLP_TILE = 1024
ADA_COLS = 2048
ROW_BLOCK = 256
SUBLANES = 8
STAGE_ROWS, STAGE_COLS = 512, 1024
STAGE_SLOTS = 4
VMEM_LIMIT_BYTES = 56 * 1024 * 1024

_BF16 = jnp.bfloat16
_F32 = jnp.float32


def _const_spec(shape):
    nd = len(shape)
    return pl.BlockSpec(shape, lambda *_: (0,) * nd, pipeline_mode=pl.Buffered(1))


def _dot(a, b):
    return jnp.dot(a, b, preferred_element_type=_F32)


def _load_cast(pairs, stage, sem, on_chunk=None):
    chunks = []
    for n, (src, dst) in enumerate(pairs):
        rows, cols = src.shape
        assert rows % STAGE_ROWS == 0 and cols % STAGE_COLS == 0
        for r in range(0, rows, STAGE_ROWS):
            for c in range(0, cols, STAGE_COLS):
                window = (pl.ds(r, STAGE_ROWS), pl.ds(c, STAGE_COLS))
                chunks.append((src.at[window], dst.at[window], (n, r, c)))
    copies = [
        pltpu.make_async_copy(src, stage.at[i % STAGE_SLOTS], sem.at[i % STAGE_SLOTS])
        for i, (src, _, _) in enumerate(chunks)]
    for copy in copies[:STAGE_SLOTS]:
        copy.start()
    for i, (_, dst, where) in enumerate(chunks):
        copies[i].wait()
        chunk = stage[i % STAGE_SLOTS]
        dst[...] = chunk.astype(_BF16)
        if on_chunk is not None:
            on_chunk(*where, chunk)
        if i + STAGE_SLOTS < len(copies):
            copies[i + STAGE_SLOTS].start()


def _first_step():
    return jnp.logical_and(pl.program_id(0) == 0, pl.program_id(1) == 0)


def _ada_kernel(c_ref, w_ref, b_ref, o_ref):
    mod = _dot(jax.nn.silu(c_ref[...]), w_ref[...]) + b_ref[...]
    for b in range(mod.shape[0]):
        o_ref[b] = mod[b:b + 1, :]


def _ada(c, w_ada, b_ada):
    bsz, d = c.shape
    n = w_ada.shape[1]
    assert n % ADA_COLS == 0
    return pl.pallas_call(
        _ada_kernel,
        out_shape=jax.ShapeDtypeStruct((bsz, 1, n), _F32),
        grid=(n // ADA_COLS,),
        in_specs=[
            pl.BlockSpec((bsz, d), lambda j: (0, 0)),
            pl.BlockSpec((d, ADA_COLS), lambda j: (0, j)),
            pl.BlockSpec((1, ADA_COLS), lambda j: (0, j)),
        ],
        out_specs=pl.BlockSpec((bsz, 1, ADA_COLS), lambda j: (0, 0, j)),
        compiler_params=pltpu.CompilerParams(
            dimension_semantics=("arbitrary",),
            vmem_limit_bytes=VMEM_LIMIT_BYTES),
        name="ada",
    )(c, w_ada, b_ada.reshape(1, n))


def _rmsnorm_mod(t, gain, shift):
    ms = jnp.mean(t * t, axis=-1, keepdims=True)
    return t * lax.rsqrt(ms + EPS) * gain + shift


def _log_gamma(h):
    return math.log1p(-(2.0 ** (-5.0 - h)))


def _block_decay(h):
    return math.exp(ROW_BLOCK * _log_gamma(h))


def _fill_decay_tables(dmask_ref, qdec_ref, kdec_ref):
    rb = ROW_BLOCK
    n = lax.broadcasted_iota(jnp.int32, (rb, rb), 0)
    m = lax.broadcasted_iota(jnp.int32, (rb, rb), 1)
    cn = lax.shift_right_logical(n, CHUNK_SHIFT)
    cm = lax.shift_right_logical(m, CHUNK_SHIFT)
    diff = (n - m).astype(_F32)
    expo = jnp.where(cn == cm, jnp.abs(diff), diff)
    nq = lax.broadcasted_iota(jnp.int32, (rb, RET_QK_DIM), 0).astype(_F32)
    mk = lax.broadcasted_iota(jnp.int32, (1, rb), 1).astype(_F32)
    for h in range(RET_HEADS):
        log_g = _log_gamma(h)
        dmask_ref[h] = jnp.where(cm <= cn, jnp.exp(expo * log_g), 0.0)
        qdec_ref[h] = jnp.exp((nq + 1.0) * log_g)
        kdec_ref[h] = jnp.exp((rb - 1.0 - mk) * log_g)


def _mixer_kernel(x_ref, mod_ref, pos_ref, n1w_ref, gnw_ref, convw_ref,
                  invf_ref,
                  w_in_hbm, w_ro_hbm, w_co_hbm, w_o_hbm,
                  h_ref,
                  state_ref, zbuf_ref, gated_ref, cbuf_ref,
                  dmask_ref, qdec_ref, kdec_ref,
                  w_in_ref, w_ro_ref, w_co_ref, w_o_ref, w_kt_ref,
                  stage, sem, *, d_model):
    tt = x_ref.shape[1]
    d = d_model
    qk = RET_HEADS * RET_QK_DIM
    rv = RET_HEADS * RET_V_DIM
    o_q, o_k, o_v = 0, qk, 2 * qk
    o_og = o_v + rv
    o_cb = o_og + rv
    o_cc = o_cb + d
    o_cx = o_cc + d
    o_ga = o_cx + d
    o_gb = o_ga + d

    @pl.when(_first_step())
    def _():
        def keep_k_transposed(src_index, r, c, chunk):
            if src_index == 0 and c <= o_k < c + STAGE_COLS:
                kcols = chunk[:, o_k - c:o_k - c + qk]
                w_kt_ref[:, r:r + STAGE_ROWS] = kcols.T.astype(_BF16)

        _load_cast([(w_in_hbm, w_in_ref), (w_ro_hbm, w_ro_ref),
                    (w_co_hbm, w_co_ref), (w_o_hbm, w_o_ref)], stage, sem,
                   on_chunk=keep_k_transposed)
        _fill_decay_tables(dmask_ref, qdec_ref, kdec_ref)

    @pl.when(pl.program_id(1) == 0)
    def _():
        state_ref[...] = jnp.zeros_like(state_ref)
        zbuf_ref[0:SUBLANES, :] = jnp.zeros((SUBLANES, d), _F32)

    sh1, sc1, g1 = (mod_ref[0, :, i * d:(i + 1) * d] for i in range(3))
    gain1 = n1w_ref[...] * (1.0 + sc1)
    kscale = RET_QK_DIM ** -0.5
    half = RET_QK_DIM // 2
    cw = d // RET_HEADS

    def project(r0):
        rows = slice(r0, r0 + ROW_BLOCK)
        ub = _rmsnorm_mod(x_ref[0, rows, :], gain1, sh1).astype(_BF16)
        pos = pos_ref[pl.ds(pl.program_id(0), 1), rows]
        ang_t = invf_ref[...] * pos.astype(_F32)
        cos_h = jnp.cos(ang_t)
        sin_h = jnp.sin(ang_t)
        cos_t = jnp.concatenate([cos_h, cos_h], axis=0)
        sin_t = jnp.concatenate([-sin_h, sin_h], axis=0)
        q = _dot(ub, w_in_ref[:, o_q:o_q + qk])
        k_t = lax.dot_general(w_kt_ref[...], ub, (((1,), (1,)), ((), ())),
                              preferred_element_type=_F32)
        v = _dot(ub, w_in_ref[:, o_v:o_v + rv])
        og = _dot(ub, w_in_ref[:, o_og:o_og + rv])
        return dict(ub=ub, q=q, k_t=k_t, v=v, og=og, cosf=cos_t.T, sinf=sin_t.T,
                    cosk_t=cos_t * kscale, sink_t=sin_t * kscale)

    def mix(r0, p):
        rows = slice(r0, r0 + ROW_BLOCK)

        def proj(lo, width):
            return _dot(p["ub"], w_in_ref[:, lo:lo + width])

        z0 = SUBLANES + r0
        sig_a, sig_b = [], []
        for h in range(RET_HEADS):
            qcols = slice(h * RET_QK_DIM, (h + 1) * RET_QK_DIM)
            vcols = slice(h * RET_V_DIM, (h + 1) * RET_V_DIM)
            ccols = slice(h * cw, (h + 1) * cw)
            qh = p["q"][:, qcols]
            kh_t = p["k_t"][qcols, :]
            kh_swap = jnp.concatenate([kh_t[half:], kh_t[:half]], axis=0)
            vh = p["v"][:, vcols].astype(_BF16)
            qr = qh * p["cosf"] + pltpu.roll(qh, half, 1) * p["sinf"]
            kr_t = kh_t * p["cosk_t"] + kh_swap * p["sink_t"]
            s = _dot(qr.astype(_BF16), kr_t.astype(_BF16)) * dmask_ref[h]
            z = proj(o_cc + h * cw, cw) * proj(o_cx + h * cw, cw)
            zbuf_ref[z0:z0 + ROW_BLOCK, ccols] = z
            st = state_ref[h]
            o = _dot(
                jnp.concatenate([s.astype(_BF16), (qr * qdec_ref[h]).astype(_BF16)], axis=1),
                jnp.concatenate([vh, st.astype(_BF16)], axis=0))
            cb = proj(o_cb + h * cw, cw)
            sig_a.append(jax.nn.sigmoid(proj(o_ga + h * cw, cw)))
            state_ref[h] = _block_decay(h) * st + _dot(
                (kr_t * kdec_ref[h]).astype(_BF16), vh)
            sig_b.append(jax.nn.sigmoid(proj(o_gb + h * cw, cw)))
            mu = jnp.mean(o, axis=-1, keepdims=True)
            oc = o - mu
            var = jnp.mean(oc * oc, axis=-1, keepdims=True)
            y = oc * lax.rsqrt(var + EPS) * gnw_ref[:, vcols]
            gated_ref[rows, vcols] = (jax.nn.silu(p["og"][:, vcols]) * y).astype(_BF16)
            z1 = zbuf_ref[z0 - 1:z0 - 1 + ROW_BLOCK, ccols]
            z2 = zbuf_ref[z0 - 2:z0 - 2 + ROW_BLOCK, ccols]
            conv = (z2 * convw_ref[0:1, ccols] + z1 * convw_ref[1:2, ccols]
                    + z * convw_ref[2:3, ccols])
            cbuf_ref[rows, ccols] = (cb * conv).astype(_BF16)
        y_ret = _dot(gated_ref[rows, :], w_ro_ref[...])
        y_conv = _dot(cbuf_ref[rows, :], w_co_ref[...])
        return (jnp.concatenate(sig_a, axis=1) * y_ret
                + jnp.concatenate(sig_b, axis=1) * y_conv)

    def finish(r0, merged):
        rows = slice(r0, r0 + ROW_BLOCK)
        h_ref[0, rows, :] = x_ref[0, rows, :] + g1 * _dot(merged.astype(_BF16), w_o_ref[...])

    for r0 in range(0, tt, ROW_BLOCK):
        finish(r0, mix(r0, project(r0)))
    zbuf_ref[0:SUBLANES, :] = zbuf_ref[tt:tt + SUBLANES, :]


def _mixer(x, mod, positions, norm1_w, ret_gn_w, conv_w, w_in, w_ret_out,
           w_conv_out, w_o):
    bsz, seq, d = x.shape
    tt = MIXER_TILE
    rv = RET_HEADS * RET_V_DIM
    qk = RET_HEADS * RET_QK_DIM
    half = RET_QK_DIM // 2
    inv_freq = ROPE_BASE ** (-jnp.arange(0, RET_QK_DIM, 2, dtype=_F32) / RET_QK_DIM)
    invf = inv_freq.reshape(half, 1)
    in_width = w_in.shape[1]

    tile_spec = pl.BlockSpec((1, tt, d), lambda b, t: (b, t, 0))
    return pl.pallas_call(
        functools.partial(_mixer_kernel, d_model=d),
        out_shape=jax.ShapeDtypeStruct((bsz, seq, d), _F32),
        grid=(bsz, seq // tt),
        in_specs=[
            tile_spec,
            pl.BlockSpec((1, 1, N_MOD * d), lambda b, t: (b, 0, 0)),
            pl.BlockSpec((bsz, tt), lambda b, t: (0, t)),
            _const_spec((1, d)),
            _const_spec((1, rv)),
            _const_spec((CONV_K, d)),
            _const_spec((half, 1)),
            pl.BlockSpec(memory_space=pl.ANY),
            pl.BlockSpec(memory_space=pl.ANY),
            pl.BlockSpec(memory_space=pl.ANY),
            pl.BlockSpec(memory_space=pl.ANY),
        ],
        out_specs=tile_spec,
        scratch_shapes=[
            pltpu.VMEM((RET_HEADS, RET_QK_DIM, RET_V_DIM), _F32),
            pltpu.VMEM((SUBLANES + tt, d), _F32),
            pltpu.VMEM((tt, rv), _BF16),
            pltpu.VMEM((tt, d), _BF16),
            pltpu.VMEM((RET_HEADS, ROW_BLOCK, ROW_BLOCK), _F32),
            pltpu.VMEM((RET_HEADS, ROW_BLOCK, RET_QK_DIM), _F32),
            pltpu.VMEM((RET_HEADS, 1, ROW_BLOCK), _F32),
            pltpu.VMEM((d, in_width), _BF16),
            pltpu.VMEM((rv, d), _BF16),
            pltpu.VMEM((d, d), _BF16),
            pltpu.VMEM((d, d), _BF16),
            pltpu.VMEM((qk, d), _BF16),
            pltpu.VMEM((STAGE_SLOTS, STAGE_ROWS, STAGE_COLS), _F32),
            pltpu.SemaphoreType.DMA((STAGE_SLOTS,)),
        ],
        compiler_params=pltpu.CompilerParams(
            dimension_semantics=("arbitrary", "arbitrary"),
            vmem_limit_bytes=VMEM_LIMIT_BYTES),
        name="mixer",
    )(x, mod, positions, norm1_w.reshape(1, d),
      ret_gn_w.reshape(1, rv), conv_w, invf,
      w_in, w_ret_out, w_conv_out, w_o)


def _mlp_kernel(h_ref, mod_ref, n2w_ref, fw_ref, w1_hbm, w2_hbm, o_ref,
                w1_ref, w2_ref, stage, sem, *, final_norm):
    @pl.when(_first_step())
    def _():
        _load_cast([(w1_hbm, w1_ref), (w2_hbm, w2_ref)], stage, sem)

    d = h_ref.shape[2]
    sh2, sc2, g2 = (mod_ref[0, :, i * d:(i + 1) * d] for i in range(3, 6))
    gain2 = n2w_ref[...] * (1.0 + sc2)
    def up(r0):
        h = h_ref[0, r0:r0 + ROW_BLOCK, :]
        u2 = _rmsnorm_mod(h, gain2, sh2).astype(_BF16)
        return jnp.square(jnp.maximum(_dot(u2, w1_ref[...]), 0.0)).astype(_BF16)

    def down(r0, hid):
        h2 = h_ref[0, r0:r0 + ROW_BLOCK, :] + g2 * _dot(hid, w2_ref[...])
        if final_norm:
            ms = jnp.mean(h2 * h2, axis=-1, keepdims=True)
            h2 = h2 * lax.rsqrt(ms + EPS) * fw_ref[...]
        o_ref[0, r0:r0 + ROW_BLOCK, :] = h2

    starts = list(range(0, h_ref.shape[1], ROW_BLOCK))
    hid = up(starts[0])
    for r0, r_next in zip(starts, starts[1:] + [None]):
        hid_next = up(r_next) if r_next is not None else None
        down(r0, hid)
        hid = hid_next


def _mlp(h, mod, norm2_w, final_norm_w, w_ff1, w_ff2, final_norm):
    bsz, seq, d = h.shape
    tm = MLP_TILE
    dff = w_ff1.shape[1]
    tile_spec = pl.BlockSpec((1, tm, d), lambda b, t: (b, t, 0))
    return pl.pallas_call(
        functools.partial(_mlp_kernel, final_norm=final_norm),
        out_shape=jax.ShapeDtypeStruct((bsz, seq, d), _F32),
        grid=(bsz, seq // tm),
        in_specs=[
            tile_spec,
            pl.BlockSpec((1, 1, N_MOD * d), lambda b, t: (b, 0, 0)),
            _const_spec((1, d)),
            _const_spec((1, d)),
            pl.BlockSpec(memory_space=pl.ANY),
            pl.BlockSpec(memory_space=pl.ANY),
        ],
        out_specs=tile_spec,
        scratch_shapes=[
            pltpu.VMEM((d, dff), _BF16),
            pltpu.VMEM((dff, d), _BF16),
            pltpu.VMEM((STAGE_SLOTS, STAGE_ROWS, STAGE_COLS), _F32),
            pltpu.SemaphoreType.DMA((STAGE_SLOTS,)),
        ],
        compiler_params=pltpu.CompilerParams(
            dimension_semantics=("arbitrary", "arbitrary"),
            vmem_limit_bytes=VMEM_LIMIT_BYTES),
        name="mlp",
    )(h, mod, norm2_w.reshape(1, d), final_norm_w.reshape(1, d),
      w_ff1, w_ff2)


def kernel(x, c, positions, w_ada, b_ada, norm1_w, w_in, ret_gn_w, conv_w, w_ret_out, w_conv_out, w_o, norm2_w, w_ff1, w_ff2, final_norm_w):
    bsz, seq, d = x.shape
    depth = w_ada.shape[0]
    assert seq % MIXER_TILE == 0 and seq % MLP_TILE == 0
    assert MIXER_TILE % ROW_BLOCK == 0 and MLP_TILE % ROW_BLOCK == 0
    assert ROW_BLOCK % CHUNK == 0 and CHUNK == 1 << CHUNK_SHIFT
    h = x
    for l in range(depth):
        mod = _ada(c, w_ada[l], b_ada[l])
        h = _mixer(h, mod, positions, norm1_w[l], ret_gn_w[l], conv_w[l],
                   w_in[l], w_ret_out[l], w_conv_out[l], w_o[l])
        h = _mlp(h, mod, norm2_w[l], final_norm_w, w_ff1[l], w_ff2[l],
                 final_norm=(l == depth - 1))
    return h
```

```python
import functools
import math

import jax
import jax.numpy as jnp
from jax import lax
from jax.experimental import pallas as pl
from jax.experimental.pallas import tpu as pltpu

CHUNK = 64
CHUNK_SHIFT = 6
RET_HEADS = 4
RET_QK_DIM = 128
RET_V_DIM = 256
CONV_K = 3
N_MOD = 6
ROPE_BASE = 10000.0
EPS = 1e-6

MIXER_TILE = 512
MLP_TILE = 1024
ROW_BLOCK = 256
SUBLANES = 8
STAGE_ROWS, STAGE_COLS = 512, 1024
STAGE_SLOTS = 4
VMEM_LIMIT_BYTES = 56 * 1024 * 1024

_BF16 = jnp.bfloat16
_F32 = jnp.float32


def _const_spec(shape):
    nd = len(shape)
    return pl.BlockSpec(shape, lambda *_: (0,) * nd, pipeline_mode=pl.Buffered(1))


def _dot(a, b):
    return jnp.dot(a, b, preferred_element_type=_F32)


def _stream_chunks(srcs, stage, sem, consume):
    windows = []
    for n, src in enumerate(srcs):
        rows, cols = src.shape
        assert rows % STAGE_ROWS == 0 and cols % STAGE_COLS == 0
        for r in range(0, rows, STAGE_ROWS):
            for c in range(0, cols, STAGE_COLS):
                windows.append((n, r, c))
    copies = [
        pltpu.make_async_copy(
            srcs[n].at[pl.ds(r, STAGE_ROWS), pl.ds(c, STAGE_COLS)],
            stage.at[i % STAGE_SLOTS], sem.at[i % STAGE_SLOTS])
        for i, (n, r, c) in enumerate(windows)]
    for copy in copies[:STAGE_SLOTS]:
        copy.start()
    for i, where in enumerate(windows):
        copies[i].wait()
        consume(*where, stage[i % STAGE_SLOTS])
        if i + STAGE_SLOTS < len(copies):
            copies[i + STAGE_SLOTS].start()


def _load_cast(pairs, stage, sem, on_chunk=None):
    def consume(n, r, c, chunk):
        pairs[n][1][r:r + STAGE_ROWS, c:c + STAGE_COLS] = chunk.astype(_BF16)
        if on_chunk is not None:
            on_chunk(n, r, c, chunk)

    _stream_chunks([src for src, _ in pairs], stage, sem, consume)


def _first_step():
    return jnp.logical_and(pl.program_id(0) == 0, pl.program_id(1) == 0)


def _ada_kernel(c_ref, b_ref, w_hbm, o_ref, stage, sem):
    silu_c = jax.nn.silu(c_ref[...])
    acc = {}

    def consume(_, r, c, chunk):
        part = _dot(silu_c[:, r:r + STAGE_ROWS], chunk)
        acc[c] = acc[c] + part if c in acc else part

    _stream_chunks([w_hbm], stage, sem, consume)
    for c, mod in acc.items():
        mod = mod + b_ref[:, c:c + STAGE_COLS]
        for b in range(mod.shape[0]):
            o_ref[b, :, c:c + STAGE_COLS] = mod[b:b + 1, :]


def _ada(c, w_ada, b_ada):
    bsz, d = c.shape
    n = w_ada.shape[1]
    return pl.pallas_call(
        _ada_kernel,
        out_shape=jax.ShapeDtypeStruct((bsz, 1, n), _F32),
        in_specs=[
            pl.BlockSpec(memory_space=pltpu.VMEM),
            pl.BlockSpec(memory_space=pltpu.VMEM),
            pl.BlockSpec(memory_space=pl.ANY),
        ],
        out_specs=pl.BlockSpec(memory_space=pltpu.VMEM),
        scratch_shapes=[
            pltpu.VMEM((STAGE_SLOTS, STAGE_ROWS, STAGE_COLS), _F32),
            pltpu.SemaphoreType.DMA((STAGE_SLOTS,)),
        ],
        name="ada",
    )(c, b_ada.reshape(1, n), w_ada)


def _rmsnorm_mod(t, gain, shift):
    ms = jnp.mean(t * t, axis=-1, keepdims=True)
    return t * lax.rsqrt(ms + EPS) * gain + shift


def _log_gamma(h):
    return math.log1p(-(2.0 ** (-5.0 - h)))


def _block_decay(h):
    return math.exp(ROW_BLOCK * _log_gamma(h))


def _fill_decay_tables(dmask_ref, qdec_ref, kdec_ref):
    rb = ROW_BLOCK
    n = lax.broadcasted_iota(jnp.int32, (rb, rb), 0)
    m = lax.broadcasted_iota(jnp.int32, (rb, rb), 1)
    cn = lax.shift_right_logical(n, CHUNK_SHIFT)
    cm = lax.shift_right_logical(m, CHUNK_SHIFT)
    diff = (n - m).astype(_F32)
    expo = jnp.where(cn == cm, jnp.abs(diff), diff)
    nq = lax.broadcasted_iota(jnp.int32, (rb, RET_QK_DIM), 0).astype(_F32)
    mk = lax.broadcasted_iota(jnp.int32, (1, rb), 1).astype(_F32)
    for h in range(RET_HEADS):
        log_g = _log_gamma(h)
        dmask_ref[h] = jnp.where(cm <= cn, jnp.exp(expo * log_g), 0.0)
        qdec_ref[h] = jnp.exp((nq + 1.0) * log_g)
        kdec_ref[h] = jnp.exp((rb - 1.0 - mk) * log_g)


def _mixer_kernel(x_ref, mod_ref, pos_ref, n1w_ref, gnw_ref, convw_ref,
                  invf_ref,
                  w_in_hbm, w_ro_hbm, w_co_hbm, w_o_hbm,
                  h_ref,
                  state_ref, zbuf_ref, gated_ref, cbuf_ref,
                  dmask_ref, qdec_ref, kdec_ref,
                  w_in_ref, w_ro_ref, w_co_ref, w_o_ref, w_kt_ref,
                  stage, sem, *, d_model):
    tt = x_ref.shape[1]
    d = d_model
    qk = RET_HEADS * RET_QK_DIM
    rv = RET_HEADS * RET_V_DIM
    o_q, o_k, o_v = 0, qk, 2 * qk
    o_og = o_v + rv
    o_cb = o_og + rv
    o_cc = o_cb + d
    o_cx = o_cc + d
    o_ga = o_cx + d
    o_gb = o_ga + d

    @pl.when(_first_step())
    def _():
        def keep_k_transposed(src_index, r, c, chunk):
            if src_index == 0 and c <= o_k < c + STAGE_COLS:
                kcols = chunk[:, o_k - c:o_k - c + qk]
                w_kt_ref[:, r:r + STAGE_ROWS] = kcols.T.astype(_BF16)

        _load_cast([(w_in_hbm, w_in_ref), (w_ro_hbm, w_ro_ref),
                    (w_co_hbm, w_co_ref), (w_o_hbm, w_o_ref)], stage, sem,
                   on_chunk=keep_k_transposed)
        _fill_decay_tables(dmask_ref, qdec_ref, kdec_ref)

    @pl.when(pl.program_id(1) == 0)
    def _():
        state_ref[...] = jnp.zeros_like(state_ref)
        zbuf_ref[0:SUBLANES, :] = jnp.zeros((SUBLANES, d), _F32)

    sh1, sc1, g1 = (mod_ref[0, :, i * d:(i + 1) * d] for i in range(3))
    gain1 = n1w_ref[...] * (1.0 + sc1)
    kscale = RET_QK_DIM ** -0.5
    half = RET_QK_DIM // 2
    cw = d // RET_HEADS

    def project(r0):
        rows = slice(r0, r0 + ROW_BLOCK)
        ub = _rmsnorm_mod(x_ref[0, rows, :], gain1, sh1).astype(_BF16)
        pos = pos_ref[pl.ds(pl.program_id(0), 1), rows]
        ang_t = invf_ref[...] * pos.astype(_F32)
        cos_h = jnp.cos(ang_t)
        sin_h = jnp.sin(ang_t)
        cos_t = jnp.concatenate([cos_h, cos_h], axis=0)
        sin_t = jnp.concatenate([-sin_h, sin_h], axis=0)
        q = _dot(ub, w_in_ref[:, o_q:o_q + qk])
        k_t = lax.dot_general(w_kt_ref[...], ub, (((1,), (1,)), ((), ())),
                              preferred_element_type=_F32)
        v = _dot(ub, w_in_ref[:, o_v:o_v + rv])
        og = _dot(ub, w_in_ref[:, o_og:o_og + rv])
        return dict(ub=ub, q=q, k_t=k_t, v=v, og=og, cosf=cos_t.T, sinf=sin_t.T,
                    cosk_t=cos_t * kscale, sink_t=sin_t * kscale)

    def mix(r0, p):
        rows = slice(r0, r0 + ROW_BLOCK)

        def proj(lo, width):
            return _dot(p["ub"], w_in_ref[:, lo:lo + width])

        z0 = SUBLANES + r0
        sig_a, sig_b = [], []
        for h in range(RET_HEADS):
            qcols = slice(h * RET_QK_DIM, (h + 1) * RET_QK_DIM)
            vcols = slice(h * RET_V_DIM, (h + 1) * RET_V_DIM)
            ccols = slice(h * cw, (h + 1) * cw)
            qh = p["q"][:, qcols]
            kh_t = p["k_t"][qcols, :]
            kh_swap = jnp.concatenate([kh_t[half:], kh_t[:half]], axis=0)
            vh = p["v"][:, vcols].astype(_BF16)
            qr = qh * p["cosf"] + pltpu.roll(qh, half, 1) * p["sinf"]
            kr_t = kh_t * p["cosk_t"] + kh_swap * p["sink_t"]
            s = _dot(qr.astype(_BF16), kr_t.astype(_BF16)) * dmask_ref[h]
            z = proj(o_cc + h * cw, cw) * proj(o_cx + h * cw, cw)
            zbuf_ref[z0:z0 + ROW_BLOCK, ccols] = z
            st = state_ref[h]
            o = _dot(
                jnp.concatenate([s.astype(_BF16), (qr * qdec_ref[h]).astype(_BF16)], axis=1),
                jnp.concatenate([vh, st.astype(_BF16)], axis=0))
            cb = proj(o_cb + h * cw, cw)
            sig_a.append(jax.nn.sigmoid(proj(o_ga + h * cw, cw)))
            state_ref[h] = _block_decay(h) * st + _dot(
                (kr_t * kdec_ref[h]).astype(_BF16), vh)
            sig_b.append(jax.nn.sigmoid(proj(o_gb + h * cw, cw)))
            mu = jnp.mean(o, axis=-1, keepdims=True)
            oc = o - mu
            var = jnp.mean(oc * oc, axis=-1, keepdims=True)
            y = oc * lax.rsqrt(var + EPS) * gnw_ref[:, vcols]
            gated_ref[rows, vcols] = (jax.nn.silu(p["og"][:, vcols]) * y).astype(_BF16)
            z1 = zbuf_ref[z0 - 1:z0 - 1 + ROW_BLOCK, ccols]
            z2 = zbuf_ref[z0 - 2:z0 - 2 + ROW_BLOCK, ccols]
            conv = (z2 * convw_ref[0:1, ccols] + z1 * convw_ref[1:2, ccols]
                    + z * convw_ref[2:3, ccols])
            cbuf_ref[rows, ccols] = (cb * conv).astype(_BF16)
        y_ret = _dot(gated_ref[rows, :], w_ro_ref[...])
        y_conv = _dot(cbuf_ref[rows, :], w_co_ref[...])
        return (jnp.concatenate(sig_a, axis=1) * y_ret
                + jnp.concatenate(sig_b, axis=1) * y_conv)

    def finish(r0, merged):
        rows = slice(r0, r0 + ROW_BLOCK)
        h_ref[0, rows, :] = x_ref[0, rows, :] + g1 * _dot(merged.astype(_BF16), w_o_ref[...])

    for r0 in range(0, tt, ROW_BLOCK):
        finish(r0, mix(r0, project(r0)))
    zbuf_ref[0:SUBLANES, :] = zbuf_ref[tt:tt + SUBLANES, :]


def _mixer(x, mod, positions, norm1_w, ret_gn_w, conv_w, w_in, w_ret_out,
           w_conv_out, w_o):
    bsz, seq, d = x.shape
    tt = MIXER_TILE
    rv = RET_HEADS * RET_V_DIM
    qk = RET_HEADS * RET_QK_DIM
    half = RET_QK_DIM // 2
    inv_freq = ROPE_BASE ** (-jnp.arange(0, RET_QK_DIM, 2, dtype=_F32) / RET_QK_DIM)
    invf = inv_freq.reshape(half, 1)
    in_width = w_in.shape[1]

    tile_spec = pl.BlockSpec((1, tt, d), lambda b, t: (b, t, 0))
    return pl.pallas_call(
        functools.partial(_mixer_kernel, d_model=d),
        out_shape=jax.ShapeDtypeStruct((bsz, seq, d), _F32),
        grid=(bsz, seq // tt),
        in_specs=[
            tile_spec,
            pl.BlockSpec((1, 1, N_MOD * d), lambda b, t: (b, 0, 0)),
            pl.BlockSpec((bsz, tt), lambda b, t: (0, t)),
            _const_spec((1, d)),
            _const_spec((1, rv)),
            _const_spec((CONV_K, d)),
            _const_spec((half, 1)),
            pl.BlockSpec(memory_space=pl.ANY),
            pl.BlockSpec(memory_space=pl.ANY),
            pl.BlockSpec(memory_space=pl.ANY),
            pl.BlockSpec(memory_space=pl.ANY),
        ],
        out_specs=tile_spec,
        scratch_shapes=[
            pltpu.VMEM((RET_HEADS, RET_QK_DIM, RET_V_DIM), _F32),
            pltpu.VMEM((SUBLANES + tt, d), _F32),
            pltpu.VMEM((tt, rv), _BF16),
            pltpu.VMEM((tt, d), _BF16),
            pltpu.VMEM((RET_HEADS, ROW_BLOCK, ROW_BLOCK), _F32),
            pltpu.VMEM((RET_HEADS, ROW_BLOCK, RET_QK_DIM), _F32),
            pltpu.VMEM((RET_HEADS, 1, ROW_BLOCK), _F32),
            pltpu.VMEM((d, in_width), _BF16),
            pltpu.VMEM((rv, d), _BF16),
            pltpu.VMEM((d, d), _BF16),
            pltpu.VMEM((d, d), _BF16),
            pltpu.VMEM((qk, d), _BF16),
            pltpu.VMEM((STAGE_SLOTS, STAGE_ROWS, STAGE_COLS), _F32),
            pltpu.SemaphoreType.DMA((STAGE_SLOTS,)),
        ],
        compiler_params=pltpu.CompilerParams(
            dimension_semantics=("arbitrary", "arbitrary"),
            vmem_limit_bytes=VMEM_LIMIT_BYTES),
        name="mixer",
    )(x, mod, positions, norm1_w.reshape(1, d),
      ret_gn_w.reshape(1, rv), conv_w, invf,
      w_in, w_ret_out, w_conv_out, w_o)


def _mlp_kernel(h_ref, mod_ref, n2w_ref, fw_ref, w1_hbm, w2_hbm, o_ref,
                w1_ref, w2_ref, stage, sem, *, final_norm):
    @pl.when(_first_step())
    def _():
        _load_cast([(w1_hbm, w1_ref), (w2_hbm, w2_ref)], stage, sem)

    d = h_ref.shape[2]
    sh2, sc2, g2 = (mod_ref[0, :, i * d:(i + 1) * d] for i in range(3, 6))
    gain2 = n2w_ref[...] * (1.0 + sc2)
    def up(r0):
        h = h_ref[0, r0:r0 + ROW_BLOCK, :]
        u2 = _rmsnorm_mod(h, gain2, sh2).astype(_BF16)
        return jnp.square(jnp.maximum(_dot(u2, w1_ref[...]), 0.0)).astype(_BF16)

    def down(r0, hid):
        h2 = h_ref[0, r0:r0 + ROW_BLOCK, :] + g2 * _dot(hid, w2_ref[...])
        if final_norm:
            ms = jnp.mean(h2 * h2, axis=-1, keepdims=True)
            h2 = h2 * lax.rsqrt(ms + EPS) * fw_ref[...]
        o_ref[0, r0:r0 + ROW_BLOCK, :] = h2

    starts = list(range(0, h_ref.shape[1], ROW_BLOCK))
    hid = up(starts[0])
    for r0, r_next in zip(starts, starts[1:] + [None]):
        hid_next = up(r_next) if r_next is not None else None
        down(r0, hid)
        hid = hid_next


def _mlp(h, mod, norm2_w, final_norm_w, w_ff1, w_ff2, final_norm):
    bsz, seq, d = h.shape
    tm = MLP_TILE
    dff = w_ff1.shape[1]
    tile_spec = pl.BlockSpec((1, tm, d), lambda b, t: (b, t, 0))
    return pl.pallas_call(
        functools.partial(_mlp_kernel, final_norm=final_norm),
        out_shape=jax.ShapeDtypeStruct((bsz, seq, d), _F32),
        grid=(bsz, seq // tm),
        in_specs=[
            tile_spec,
            pl.BlockSpec((1, 1, N_MOD * d), lambda b, t: (b, 0, 0)),
            _const_spec((1, d)),
            _const_spec((1, d)),
            pl.BlockSpec(memory_space=pl.ANY),
            pl.BlockSpec(memory_space=pl.ANY),
        ],
        out_specs=tile_spec,
        scratch_shapes=[
            pltpu.VMEM((d, dff), _BF16),
            pltpu.VMEM((dff, d), _BF16),
            pltpu.VMEM((STAGE_SLOTS, STAGE_ROWS, STAGE_COLS), _F32),
            pltpu.SemaphoreType.DMA((STAGE_SLOTS,)),
        ],
        compiler_params=pltpu.CompilerParams(
            dimension_semantics=("arbitrary", "arbitrary"),
            vmem_limit_bytes=VMEM_LIMIT_BYTES),
        name="mlp",
    )(h, mod, norm2_w.reshape(1, d), final_norm_w.reshape(1, d),
      w_ff1, w_ff2)


def kernel(x, c, positions, w_ada, b_ada, norm1_w, w_in, ret_gn_w, conv_w, w_ret_out, w_conv_out, w_o, norm2_w, w_ff1, w_ff2, final_norm_w):
    bsz, seq, d = x.shape
    depth = w_ada.shape[0]
    assert seq % MIXER_TILE == 0 and seq % MLP_TILE == 0
    assert MIXER_TILE % ROW_BLOCK == 0 and MLP_TILE % ROW_BLOCK == 0
    assert ROW_BLOCK % CHUNK == 0 and CHUNK == 1 << CHUNK_SHIFT
    h = x
    for l in range(depth):
        mod = _ada(c, w_ada[l], b_ada[l])
        h = _mixer(h, mod, positions, norm1_w[l], ret_gn_w[l], conv_w[l],
                   w_in[l], w_ret_out[l], w_conv_out[l], w_o[l])
        h = _mlp(h, mod, norm2_w[l], final_norm_w, w_ff1[l], w_ff2[l],
                 final_norm=(l == depth - 1))
    return h
```

```python
import functools
import math

import jax
import jax.numpy as jnp
from jax import lax
from jax.experimental import pallas as pl
from jax.experimental.pallas import tpu as pltpu

CHUNK = 64
CHUNK_SHIFT = 6
RET_HEADS = 4
RET_QK_DIM = 128
RET_V_DIM = 256
CONV_K = 3
N_MOD = 6
ROPE_BASE = 10000.0
EPS = 1e-6

MIXER_TILE = 512
MLP_TILE = 1024
ROW_BLOCK = 256
MLP_ROW_BLOCK = 512
SUBLANES = 8
STAGE_ROWS, STAGE_COLS = 512, 1024
STAGE_SLOTS = 4
VMEM_LIMIT_BYTES = 56 * 1024 * 1024

_BF16 = jnp.bfloat16
_F32 = jnp.float32


def _const_spec(shape):
    nd = len(shape)
    return pl.BlockSpec(shape, lambda *_: (0,) * nd, pipeline_mode=pl.Buffered(1))


def _dot(a, b):
    return jnp.dot(a, b, preferred_element_type=_F32)


def _stream_chunks(srcs, stage, sem, consume):
    windows = []
    for n, src in enumerate(srcs):
        rows, cols = src.shape
        assert rows % STAGE_ROWS == 0 and cols % STAGE_COLS == 0
        for r in range(0, rows, STAGE_ROWS):
            for c in range(0, cols, STAGE_COLS):
                windows.append((n, r, c))
    copies = [
        pltpu.make_async_copy(
            srcs[n].at[pl.ds(r, STAGE_ROWS), pl.ds(c, STAGE_COLS)],
            stage.at[i % STAGE_SLOTS], sem.at[i % STAGE_SLOTS])
        for i, (n, r, c) in enumerate(windows)]
    for copy in copies[:STAGE_SLOTS]:
        copy.start()
    for i, where in enumerate(windows):
        copies[i].wait()
        consume(*where, stage[i % STAGE_SLOTS])
        if i + STAGE_SLOTS < len(copies):
            copies[i + STAGE_SLOTS].start()


def _load_cast(pairs, stage, sem, on_chunk=None):
    def consume(n, r, c, chunk):
        pairs[n][1][r:r + STAGE_ROWS, c:c + STAGE_COLS] = chunk.astype(_BF16)
        if on_chunk is not None:
            on_chunk(n, r, c, chunk)

    _stream_chunks([src for src, _ in pairs], stage, sem, consume)


def _first_step():
    return jnp.logical_and(pl.program_id(0) == 0, pl.program_id(1) == 0)


def _ada_kernel(c_ref, b_ref, w_hbm, o_ref, stage, sem):
    silu_c = jax.nn.silu(c_ref[...])
    acc = {}

    def consume(_, r, c, chunk):
        part = _dot(silu_c[:, r:r + STAGE_ROWS], chunk)
        acc[c] = acc[c] + part if c in acc else part

    _stream_chunks([w_hbm], stage, sem, consume)
    for c, mod in acc.items():
        mod = mod + b_ref[:, c:c + STAGE_COLS]
        for b in range(mod.shape[0]):
            o_ref[b, :, c:c + STAGE_COLS] = mod[b:b + 1, :]


def _ada(c, w_ada, b_ada):
    bsz, d = c.shape
    n = w_ada.shape[1]
    return pl.pallas_call(
        _ada_kernel,
        out_shape=jax.ShapeDtypeStruct((bsz, 1, n), _F32),
        in_specs=[
            pl.BlockSpec(memory_space=pltpu.VMEM),
            pl.BlockSpec(memory_space=pltpu.VMEM),
            pl.BlockSpec(memory_space=pl.ANY),
        ],
        out_specs=pl.BlockSpec(memory_space=pltpu.VMEM),
        scratch_shapes=[
            pltpu.VMEM((STAGE_SLOTS, STAGE_ROWS, STAGE_COLS), _F32),
            pltpu.SemaphoreType.DMA((STAGE_SLOTS,)),
        ],
        name="ada",
    )(c, b_ada.reshape(1, n), w_ada)


def _rmsnorm_mod(t, gain, shift):
    ms = jnp.mean(t * t, axis=-1, keepdims=True)
    return t * lax.rsqrt(ms + EPS) * gain + shift


def _log_gamma(h):
    return math.log1p(-(2.0 ** (-5.0 - h)))


def _block_decay(h):
    return math.exp(ROW_BLOCK * _log_gamma(h))


def _fill_decay_tables(dmask_ref, qdec_ref, kdec_ref):
    rb = ROW_BLOCK
    n = lax.broadcasted_iota(jnp.int32, (rb, rb), 0)
    m = lax.broadcasted_iota(jnp.int32, (rb, rb), 1)
    cn = lax.shift_right_logical(n, CHUNK_SHIFT)
    cm = lax.shift_right_logical(m, CHUNK_SHIFT)
    diff = (n - m).astype(_F32)
    expo = jnp.where(cn == cm, jnp.abs(diff), diff)
    nq = lax.broadcasted_iota(jnp.int32, (rb, RET_QK_DIM), 0).astype(_F32)
    mk = lax.broadcasted_iota(jnp.int32, (1, rb), 1).astype(_F32)
    for h in range(RET_HEADS):
        log_g = _log_gamma(h)
        dmask_ref[h] = jnp.where(cm <= cn, jnp.exp(expo * log_g), 0.0)
        qdec_ref[h] = jnp.exp((nq + 1.0) * log_g)
        kdec_ref[h] = jnp.exp((rb - 1.0 - mk) * log_g)


def _mixer_kernel(x_ref, mod_ref, pos_ref, n1w_ref, gnw_ref, convw_ref,
                  invf_ref,
                  w_in_hbm, w_ro_hbm, w_co_hbm, w_o_hbm,
                  h_ref,
                  state_ref, zbuf_ref, gated_ref, cbuf_ref,
                  dmask_ref, qdec_ref, kdec_ref,
                  w_in_ref, w_ro_ref, w_co_ref, w_o_ref, w_kt_ref,
                  stage, sem, *, d_model):
    tt = x_ref.shape[1]
    d = d_model
    qk = RET_HEADS * RET_QK_DIM
    rv = RET_HEADS * RET_V_DIM
    o_q, o_k, o_v = 0, qk, 2 * qk
    o_og = o_v + rv
    o_cb = o_og + rv
    o_cc = o_cb + d
    o_cx = o_cc + d
    o_ga = o_cx + d
    o_gb = o_ga + d

    @pl.when(_first_step())
    def _():
        def keep_k_transposed(src_index, r, c, chunk):
            if src_index == 0 and c <= o_k < c + STAGE_COLS:
                kcols = chunk[:, o_k - c:o_k - c + qk]
                w_kt_ref[:, r:r + STAGE_ROWS] = kcols.T.astype(_BF16)

        _load_cast([(w_in_hbm, w_in_ref), (w_ro_hbm, w_ro_ref),
                    (w_co_hbm, w_co_ref), (w_o_hbm, w_o_ref)], stage, sem,
                   on_chunk=keep_k_transposed)
        _fill_decay_tables(dmask_ref, qdec_ref, kdec_ref)

    @pl.when(pl.program_id(1) == 0)
    def _():
        state_ref[...] = jnp.zeros_like(state_ref)
        zbuf_ref[0:SUBLANES, :] = jnp.zeros((SUBLANES, d), _F32)

    sh1, sc1, g1 = (mod_ref[0, :, i * d:(i + 1) * d] for i in range(3))
    gain1 = n1w_ref[...] * (1.0 + sc1)
    kscale = RET_QK_DIM ** -0.5
    half = RET_QK_DIM // 2
    cw = d // RET_HEADS

    def project(r0):
        rows = slice(r0, r0 + ROW_BLOCK)
        ub = _rmsnorm_mod(x_ref[0, rows, :], gain1, sh1).astype(_BF16)
        pos = pos_ref[pl.ds(pl.program_id(0), 1), rows]
        ang_t = invf_ref[...] * pos.astype(_F32)
        cos_h = jnp.cos(ang_t)
        sin_h = jnp.sin(ang_t)
        cos_t = jnp.concatenate([cos_h, cos_h], axis=0)
        sin_t = jnp.concatenate([-sin_h, sin_h], axis=0)
        q = _dot(ub, w_in_ref[:, o_q:o_q + qk])
        k_t = lax.dot_general(w_kt_ref[...], ub, (((1,), (1,)), ((), ())),
                              preferred_element_type=_F32)
        v = _dot(ub, w_in_ref[:, o_v:o_v + rv])
        og = _dot(ub, w_in_ref[:, o_og:o_og + rv])
        return dict(ub=ub, q=q, k_t=k_t, v=v, og=og, cosf=cos_t.T, sinf=sin_t.T,
                    cosk_t=cos_t * kscale, sink_t=sin_t * kscale)

    def mix(r0, p):
        rows = slice(r0, r0 + ROW_BLOCK)

        def proj(lo, width):
            return _dot(p["ub"], w_in_ref[:, lo:lo + width])

        z0 = SUBLANES + r0
        sig_a, sig_b = [], []
        for h in range(RET_HEADS):
            qcols = slice(h * RET_QK_DIM, (h + 1) * RET_QK_DIM)
            vcols = slice(h * RET_V_DIM, (h + 1) * RET_V_DIM)
            ccols = slice(h * cw, (h + 1) * cw)
            qh = p["q"][:, qcols]
            kh_t = p["k_t"][qcols, :]
            kh_swap = jnp.concatenate([kh_t[half:], kh_t[:half]], axis=0)
            vh = p["v"][:, vcols].astype(_BF16)
            qr = qh * p["cosf"] + pltpu.roll(qh, half, 1) * p["sinf"]
            kr_t = kh_t * p["cosk_t"] + kh_swap * p["sink_t"]
            s = _dot(qr.astype(_BF16), kr_t.astype(_BF16)) * dmask_ref[h]
            z = proj(o_cc + h * cw, cw) * proj(o_cx + h * cw, cw)
            zbuf_ref[z0:z0 + ROW_BLOCK, ccols] = z
            st = state_ref[h]
            o = _dot(
                jnp.concatenate([s.astype(_BF16), (qr * qdec_ref[h]).astype(_BF16)], axis=1),
                jnp.concatenate([vh, st.astype(_BF16)], axis=0))
            cb = proj(o_cb + h * cw, cw)
            sig_a.append(jax.nn.sigmoid(proj(o_ga + h * cw, cw)))
            state_ref[h] = _block_decay(h) * st + _dot(
                (kr_t * kdec_ref[h]).astype(_BF16), vh)
            sig_b.append(jax.nn.sigmoid(proj(o_gb + h * cw, cw)))
            mu = jnp.mean(o, axis=-1, keepdims=True)
            oc = o - mu
            var = jnp.mean(oc * oc, axis=-1, keepdims=True)
            y = oc * lax.rsqrt(var + EPS) * gnw_ref[:, vcols]
            gated_ref[rows, vcols] = (jax.nn.silu(p["og"][:, vcols]) * y).astype(_BF16)
            z1 = zbuf_ref[z0 - 1:z0 - 1 + ROW_BLOCK, ccols]
            z2 = zbuf_ref[z0 - 2:z0 - 2 + ROW_BLOCK, ccols]
            conv = (z2 * convw_ref[0:1, ccols] + z1 * convw_ref[1:2, ccols]
                    + z * convw_ref[2:3, ccols])
            cbuf_ref[rows, ccols] = (cb * conv).astype(_BF16)
        y_ret = _dot(gated_ref[rows, :], w_ro_ref[...])
        y_conv = _dot(cbuf_ref[rows, :], w_co_ref[...])
        return (jnp.concatenate(sig_a, axis=1) * y_ret
                + jnp.concatenate(sig_b, axis=1) * y_conv)

    def finish(r0, merged):
        rows = slice(r0, r0 + ROW_BLOCK)
        h_ref[0, rows, :] = x_ref[0, rows, :] + g1 * _dot(merged.astype(_BF16), w_o_ref[...])

    for r0 in range(0, tt, ROW_BLOCK):
        finish(r0, mix(r0, project(r0)))
    zbuf_ref[0:SUBLANES, :] = zbuf_ref[tt:tt + SUBLANES, :]


def _mixer(x, mod, positions, norm1_w, ret_gn_w, conv_w, w_in, w_ret_out,
           w_conv_out, w_o):
    bsz, seq, d = x.shape
    tt = MIXER_TILE
    rv = RET_HEADS * RET_V_DIM
    qk = RET_HEADS * RET_QK_DIM
    half = RET_QK_DIM // 2
    inv_freq = ROPE_BASE ** (-jnp.arange(0, RET_QK_DIM, 2, dtype=_F32) / RET_QK_DIM)
    invf = inv_freq.reshape(half, 1)
    in_width = w_in.shape[1]

    tile_spec = pl.BlockSpec((1, tt, d), lambda b, t: (b, t, 0))
    return pl.pallas_call(
        functools.partial(_mixer_kernel, d_model=d),
        out_shape=jax.ShapeDtypeStruct((bsz, seq, d), _F32),
        grid=(bsz, seq // tt),
        in_specs=[
            tile_spec,
            pl.BlockSpec((1, 1, N_MOD * d), lambda b, t: (b, 0, 0)),
            pl.BlockSpec((bsz, tt), lambda b, t: (0, t)),
            _const_spec((1, d)),
            _const_spec((1, rv)),
            _const_spec((CONV_K, d)),
            _const_spec((half, 1)),
            pl.BlockSpec(memory_space=pl.ANY),
            pl.BlockSpec(memory_space=pl.ANY),
            pl.BlockSpec(memory_space=pl.ANY),
            pl.BlockSpec(memory_space=pl.ANY),
        ],
        out_specs=tile_spec,
        scratch_shapes=[
            pltpu.VMEM((RET_HEADS, RET_QK_DIM, RET_V_DIM), _F32),
            pltpu.VMEM((SUBLANES + tt, d), _F32),
            pltpu.VMEM((tt, rv), _BF16),
            pltpu.VMEM((tt, d), _BF16),
            pltpu.VMEM((RET_HEADS, ROW_BLOCK, ROW_BLOCK), _F32),
            pltpu.VMEM((RET_HEADS, ROW_BLOCK, RET_QK_DIM), _F32),
            pltpu.VMEM((RET_HEADS, 1, ROW_BLOCK), _F32),
            pltpu.VMEM((d, in_width), _BF16),
            pltpu.VMEM((rv, d), _BF16),
            pltpu.VMEM((d, d), _BF16),
            pltpu.VMEM((d, d), _BF16),
            pltpu.VMEM((qk, d), _BF16),
            pltpu.VMEM((STAGE_SLOTS, STAGE_ROWS, STAGE_COLS), _F32),
            pltpu.SemaphoreType.DMA((STAGE_SLOTS,)),
        ],
        compiler_params=pltpu.CompilerParams(
            dimension_semantics=("arbitrary", "arbitrary"),
            vmem_limit_bytes=VMEM_LIMIT_BYTES),
        name="mixer",
    )(x, mod, positions, norm1_w.reshape(1, d),
      ret_gn_w.reshape(1, rv), conv_w, invf,
      w_in, w_ret_out, w_conv_out, w_o)


def _mlp_kernel(h_ref, mod_ref, n2w_ref, fw_ref, w1_hbm, w2_hbm, o_ref,
                w1_ref, w2_ref, stage, sem, *, final_norm):
    @pl.when(_first_step())
    def _():
        _load_cast([(w1_hbm, w1_ref), (w2_hbm, w2_ref)], stage, sem)

    d = h_ref.shape[2]
    sh2, sc2, g2 = (mod_ref[0, :, i * d:(i + 1) * d] for i in range(3, 6))
    gain2 = n2w_ref[...] * (1.0 + sc2)
    def up(r0):
        h = h_ref[0, r0:r0 + MLP_ROW_BLOCK, :]
        u2 = _rmsnorm_mod(h, gain2, sh2).astype(_BF16)
        return jnp.square(jnp.maximum(_dot(u2, w1_ref[...]), 0.0)).astype(_BF16)

    def down(r0, hid):
        h2 = h_ref[0, r0:r0 + MLP_ROW_BLOCK, :] + g2 * _dot(hid, w2_ref[...])
        if final_norm:
            ms = jnp.mean(h2 * h2, axis=-1, keepdims=True)
            h2 = h2 * lax.rsqrt(ms + EPS) * fw_ref[...]
        o_ref[0, r0:r0 + MLP_ROW_BLOCK, :] = h2

    starts = list(range(0, h_ref.shape[1], MLP_ROW_BLOCK))
    hid = up(starts[0])
    for r0, r_next in zip(starts, starts[1:] + [None]):
        hid_next = up(r_next) if r_next is not None else None
        down(r0, hid)
        hid = hid_next


def _mlp(h, mod, norm2_w, final_norm_w, w_ff1, w_ff2, final_norm):
    bsz, seq, d = h.shape
    tm = MLP_TILE
    dff = w_ff1.shape[1]
    tile_spec = pl.BlockSpec((1, tm, d), lambda b, t: (b, t, 0))
    return pl.pallas_call(
        functools.partial(_mlp_kernel, final_norm=final_norm),
        out_shape=jax.ShapeDtypeStruct((bsz, seq, d), _F32),
        grid=(bsz, seq // tm),
        in_specs=[
            tile_spec,
            pl.BlockSpec((1, 1, N_MOD * d), lambda b, t: (b, 0, 0)),
            _const_spec((1, d)),
            _const_spec((1, d)),
            pl.BlockSpec(memory_space=pl.ANY),
            pl.BlockSpec(memory_space=pl.ANY),
        ],
        out_specs=tile_spec,
        scratch_shapes=[
            pltpu.VMEM((d, dff), _BF16),
            pltpu.VMEM((dff, d), _BF16),
            pltpu.VMEM((STAGE_SLOTS, STAGE_ROWS, STAGE_COLS), _F32),
            pltpu.SemaphoreType.DMA((STAGE_SLOTS,)),
        ],
        compiler_params=pltpu.CompilerParams(
            dimension_semantics=("arbitrary", "arbitrary"),
            vmem_limit_bytes=VMEM_LIMIT_BYTES),
        name="mlp",
    )(h, mod, norm2_w.reshape(1, d), final_norm_w.reshape(1, d),
      w_ff1, w_ff2)


def kernel(x, c, positions, w_ada, b_ada, norm1_w, w_in, ret_gn_w, conv_w, w_ret_out, w_conv_out, w_o, norm2_w, w_ff1, w_ff2, final_norm_w):
    bsz, seq, d = x.shape
    depth = w_ada.shape[0]
    assert seq % MIXER_TILE == 0 and seq % MLP_TILE == 0
    assert MIXER_TILE % ROW_BLOCK == 0 and MLP_TILE % MLP_ROW_BLOCK == 0
    assert ROW_BLOCK % CHUNK == 0 and CHUNK == 1 << CHUNK_SHIFT
    h = x
    for l in range(depth):
        mod = _ada(c, w_ada[l], b_ada[l])
        h = _mixer(h, mod, positions, norm1_w[l], ret_gn_w[l], conv_w[l],
                   w_in[l], w_ret_out[l], w_conv_out[l], w_o[l])
        h = _mlp(h, mod, norm2_w[l], final_norm_w, w_ff1[l], w_ff2[l],
                 final_norm=(l == depth - 1))
    return h
```
